```python
import jax, jax.numpy as jnp
from jax import lax
import numpy as np

D_MODEL = 1024
BATCH = 4
SEQ = 4096
DEPTH = 4

N_MIXERS = 2
N_GLA = (DEPTH + 1) // 2
N_FOX = DEPTH // 2
N_DENSE = (DEPTH + 1) // 2
N_MOE = DEPTH // 2

GLA_HEADS = 4
GLA_DK = D_MODEL // 2
GLA_DV = D_MODEL
GLA_HK = GLA_DK // GLA_HEADS
GLA_HV = GLA_DV // GLA_HEADS
GLA_GATE_RANK = 16
GLA_GATE_TAU = 16.0
GLA_CHUNK = 64
GLA_IN_DIM = 2 * GLA_DK + 2 * GLA_DV + GLA_GATE_RANK

FOX_HEAD_DIM = 64
FOX_HEADS = D_MODEL // FOX_HEAD_DIM
FOX_Q_BLOCK = 128
FOX_IN_DIM = 4 * D_MODEL + FOX_HEADS
FOX_FORGET_BIAS = 3.0

D_FF = 7 * D_MODEL // 2
N_EXPERTS = 8
TOP_K = 2

EPS = 1e-6

kernel_name = "hybrid_gla_fox_moe_adaln"


def rms_norm(x, gain):
    xf = x.astype(jnp.float32)
    y = xf * lax.rsqrt(jnp.mean(xf * xf, axis=-1, keepdims=True) + EPS)
    return (y * gain.astype(jnp.float32)).astype(x.dtype)


def swiglu(h, w_gate, w_up, w_down):
    return (jax.nn.silu(h @ w_gate) * (h @ w_up)) @ w_down


def gla_chunked(q, k, v, log_a):
    B, H, T, dk = q.shape
    dv = v.shape[-1]
    n = T // GLA_CHUNK

    def chunks(t):
        return t.reshape(B, H, n, GLA_CHUNK, t.shape[-1]).transpose(2, 0, 1, 3, 4)

    qc, kc, vc = chunks(q), chunks(k), chunks(v)
    bc = jnp.cumsum(chunks(log_a), axis=-2)
    causal = jnp.tril(jnp.ones((GLA_CHUNK, GLA_CHUNK), dtype=bool))

    def step(S, inp):
        qi, ki, vi, bi = inp
        b_last = bi[:, :, -1, :]
        q_dec = qi * jnp.exp(bi)
        k_inv = ki * jnp.exp(-bi)
        k_to_end = ki * jnp.exp(b_last[:, :, None, :] - bi)
        attn = jnp.where(causal, jnp.einsum('bhtd,bhsd->bhts', q_dec, k_inv), 0.0)
        o = (jnp.einsum('bhts,bhsv->bhtv', attn, vi)
             + jnp.einsum('bhtd,bhdv->bhtv', q_dec, S))
        S = jnp.exp(b_last)[..., None] * S + jnp.einsum('bhsd,bhsv->bhdv', k_to_end, vi)
        return S, o

    S0 = jnp.zeros((B, H, dk, dv), jnp.float32)
    _, oc = lax.scan(step, S0, (qc, kc, vc, bc))
    return oc.transpose(1, 2, 0, 3, 4).reshape(B, H, T, dv)


def gla_mixer(h, w_in, w_gate_up, b_gate, o_norm, w_out):
    B, T, _ = h.shape
    proj = h @ w_in
    q, k, v, r, g_low = jnp.split(
        proj, [GLA_DK, 2 * GLA_DK, 2 * GLA_DK + GLA_DV, 2 * GLA_DK + 2 * GLA_DV], axis=-1)
    log_a = jax.nn.log_sigmoid((g_low @ w_gate_up + b_gate).astype(jnp.float32)) / GLA_GATE_TAU

    def heads(t, hd):
        return t.reshape(B, T, GLA_HEADS, hd).transpose(0, 2, 1, 3).astype(jnp.float32)

    o = gla_chunked(heads(q, GLA_HK) * (GLA_HK ** -0.5), heads(k, GLA_HK),
                    heads(v, GLA_HV), heads(log_a, GLA_HK))
    o = rms_norm(o, o_norm).transpose(0, 2, 1, 3).reshape(B, T, GLA_DV).astype(h.dtype)
    return (o * jax.nn.silu(r)) @ w_out


def fox_mixer(h, w_in, b_f, q_norm, k_norm, w_out):
    B, T, D = h.shape
    proj = h @ w_in
    q, k, v, g, f = jnp.split(proj, [D, 2 * D, 3 * D, 4 * D], axis=-1)
    log_f = jax.nn.log_sigmoid((f + b_f).astype(jnp.float32))
    F = jnp.cumsum(log_f, axis=1).transpose(0, 2, 1)

    def heads(t):
        return t.reshape(B, T, FOX_HEADS, FOX_HEAD_DIM).transpose(0, 2, 1, 3)

    q = rms_norm(heads(q), q_norm)
    k = rms_norm(heads(k), k_norm)
    v = heads(v)
    scale = FOX_HEAD_DIM ** -0.5
    outs = []
    for blk in range(T // FOX_Q_BLOCK):
        s0 = blk * FOX_Q_BLOCK
        s1 = s0 + FOX_Q_BLOCK
        logits = (jnp.einsum('bhqd,bhkd->bhqk', q[:, :, s0:s1], k[:, :, :s1]).astype(jnp.float32)
                  * scale + F[:, :, s0:s1, None] - F[:, :, None, :s1])
        mask = (s0 + jnp.arange(FOX_Q_BLOCK))[:, None] >= jnp.arange(s1)[None, :]
        p = jax.nn.softmax(jnp.where(mask, logits, -jnp.inf), axis=-1)
        outs.append(jnp.einsum('bhqk,bhkd->bhqd', p.astype(v.dtype), v[:, :, :s1]))
    o = jnp.concatenate(outs, axis=2).transpose(0, 2, 1, 3).reshape(B, T, D)
    return (o * jax.nn.sigmoid(g)) @ w_out


def moe_swiglu(h, w_router, b_router, w_gate, w_up, w_down):
    logits = (h @ w_router).astype(jnp.float32) + b_router.astype(jnp.float32)
    top_vals, top_idx = lax.top_k(logits, TOP_K)
    top_w = jax.nn.softmax(top_vals, axis=-1)
    combine = jnp.sum(jax.nn.one_hot(top_idx, N_EXPERTS, dtype=jnp.float32)
                      * top_w[..., None], axis=-2)
    y = jnp.zeros_like(h)
    for e in range(N_EXPERTS):
        y = y + combine[..., e:e + 1].astype(h.dtype) * swiglu(h, w_gate[e], w_up[e], w_down[e])
    return y


def setup_inputs(seed: int = 0) -> dict:
    key = jax.random.key(seed)
    ks = jax.random.split(key, 24)
    D = D_MODEL
    f32 = jnp.float32

    def nrm(k, shape, fan_in, s=1.0):
        return (s * fan_in ** -0.5) * jax.random.normal(k, shape, f32)

    def rn(k, shape):
        return jax.random.normal(k, shape, f32)

    return {
        "x": rn(ks[0], (BATCH, SEQ, D)),
        "c": rn(ks[1], (BATCH, D)),
        "ada_w": nrm(ks[2], (DEPTH, D, 6 * D), D, 0.5),
        "ada_b": 0.02 * rn(ks[3], (DEPTH, 6 * D)),
        "norm_gain": 1.0 + 0.1 * rn(ks[4], (DEPTH, 2, D)),
        "gla_w_in": nrm(ks[5], (N_GLA, D, GLA_IN_DIM), D),
        "gla_w_gate_up": nrm(ks[6], (N_GLA, GLA_GATE_RANK, GLA_DK), GLA_GATE_RANK),
        "gla_b_gate": 0.1 * rn(ks[7], (N_GLA, GLA_DK)),
        "gla_o_norm": 1.0 + 0.1 * rn(ks[8], (N_GLA, GLA_HV)),
        "gla_w_out": nrm(ks[9], (N_GLA, GLA_DV, D), GLA_DV),
        "fox_w_in": nrm(ks[10], (N_FOX, D, FOX_IN_DIM), D),
        "fox_b_f": FOX_FORGET_BIAS + 0.1 * rn(ks[11], (N_FOX, FOX_HEADS)),
        "fox_q_norm": 1.0 + 0.1 * rn(ks[12], (N_FOX, FOX_HEAD_DIM)),
        "fox_k_norm": 1.0 + 0.1 * rn(ks[13], (N_FOX, FOX_HEAD_DIM)),
        "fox_w_out": nrm(ks[14], (N_FOX, D, D), D),
        "ffn_w_gate": nrm(ks[15], (N_DENSE, D, D_FF), D),
        "ffn_w_up": nrm(ks[16], (N_DENSE, D, D_FF), D),
        "ffn_w_down": nrm(ks[17], (N_DENSE, D_FF, D), D_FF),
        "moe_w_router": nrm(ks[18], (N_MOE, D, N_EXPERTS), D),
        "moe_b_router": 0.01 * rn(ks[19], (N_MOE, N_EXPERTS)),
        "moe_w_gate": nrm(ks[20], (N_MOE, N_EXPERTS, D, D_FF), D),
        "moe_w_up": nrm(ks[21], (N_MOE, N_EXPERTS, D, D_FF), D),
        "moe_w_down": nrm(ks[22], (N_MOE, N_EXPERTS, D_FF, D), D_FF),
    }


def reference(x, c, ada_w, ada_b, norm_gain, gla_w_in, gla_w_gate_up, gla_b_gate, gla_o_norm,
              gla_w_out, fox_w_in, fox_b_f, fox_q_norm, fox_k_norm, fox_w_out, ffn_w_gate,
              ffn_w_up, ffn_w_down, moe_w_router, moe_b_router, moe_w_gate, moe_w_up,
              moe_w_down):
    cond = jax.nn.silu(c)
    for i in range(DEPTH):
        j = i // N_MIXERS
        mod = (cond @ ada_w[i] + ada_b[i])[:, None, :]
        sh1, sc1, g1, sh2, sc2, g2 = jnp.split(mod, 6, axis=-1)

        h = rms_norm(x, norm_gain[i, 0]) * (1.0 + sc1) + sh1
        if i % N_MIXERS == 0:
            mix = gla_mixer(h, gla_w_in[j], gla_w_gate_up[j], gla_b_gate[j], gla_o_norm[j],
                            gla_w_out[j])
        else:
            mix = fox_mixer(h, fox_w_in[j], fox_b_f[j], fox_q_norm[j], fox_k_norm[j],
                            fox_w_out[j])
        x = x + g1 * mix

        h = rms_norm(x, norm_gain[i, 1]) * (1.0 + sc2) + sh2
        if i % 2 == 0:
            ffn = swiglu(h, ffn_w_gate[j], ffn_w_up[j], ffn_w_down[j])
        else:
            ffn = moe_swiglu(h, moe_w_router[j], moe_b_router[j], moe_w_gate[j], moe_w_up[j],
                             moe_w_down[j])
        x = x + g2 * ffn
    return x
```

```python
import functools

import jax
import jax.numpy as jnp
from jax import lax
from jax.experimental import pallas as pl
from jax.experimental.pallas import tpu as pltpu

F32 = jnp.float32
BF16 = jnp.bfloat16

EPS = 1e-6
GLA_HEADS = 4
GLA_GATE_TAU = 16.0
GLA_CHUNK = 64
FOX_HEAD_DIM = 64
N_EXPERTS = 8
LANES = 128
VMEM_LIMIT = 56 * 1024 * 1024


def _cparams(sem):
    return pltpu.CompilerParams(dimension_semantics=sem, vmem_limit_bytes=VMEM_LIMIT)


def _dot(a, b):
    return jnp.dot(a, b, preferred_element_type=F32)


def _dot_nt(a, b):
    return lax.dot_general(a, b, (((1,), (1,)), ((), ())), preferred_element_type=F32)


def _dot_tn(a, b):
    return lax.dot_general(a, b, (((0,), (0,)), ((), ())), preferred_element_type=F32)


def _split3(a):
    a1 = a.astype(BF16)
    r1 = a - a1.astype(F32)
    a2 = r1.astype(BF16)
    a3 = (r1 - a2.astype(F32)).astype(BF16)
    return a1, a2, a3


def _exact_left_dot(m01, a):
    a1, a2, a3 = _split3(a)
    return _dot(m01, a1) + _dot(m01, a2) + _dot(m01, a3)


def _log_sigmoid(x):
    return jnp.minimum(x, 0.0) - jnp.log1p(jnp.exp(-jnp.abs(x)))


def _sigmoid(x):
    return 1.0 / (1.0 + jnp.exp(-x))


def _silu(x):
    return x * _sigmoid(x)


def _norm_mod(x, gain, scale, shift):
    y = x * lax.rsqrt(jnp.mean(x * x, axis=-1, keepdims=True) + EPS)
    return (y * gain) * (1.0 + scale) + shift


def _ada_body(c_ref, w_ref, b_ref, o_ref):
    cond = _silu(c_ref[...]).astype(BF16)
    o_ref[0] = _dot(cond, w_ref[0].astype(BF16)) + b_ref[0]


def ada_mods(c_pad, ada_w, ada_b, tn=1536):
    depth, d, n = ada_w.shape
    rows = c_pad.shape[0]
    return pl.pallas_call(
        _ada_body,
        out_shape=jax.ShapeDtypeStruct((depth, rows, n), F32),
        grid=(depth, n // tn),
        in_specs=[
            pl.BlockSpec((rows, d), lambda i, j: (0, 0)),
            pl.BlockSpec((1, d, tn), lambda i, j: (i, 0, j)),
            pl.BlockSpec((1, 1, tn), lambda i, j: (i, 0, j)),
        ],
        out_specs=pl.BlockSpec((1, rows, tn), lambda i, j: (i, 0, j)),
        compiler_params=_cparams(("parallel", "parallel")),
        name="ada_mods",
    )(c_pad, ada_w, ada_b.reshape(depth, 1, n))


def _gla_in_body(x_ref, gain_ref, sc_ref, sh_ref, w_ref, wg_ref, bg_ref,
                 qk_ref, v_ref, r_ref, la_ref, *, dk, dv):
    h = _norm_mod(x_ref[0], gain_ref[...], sc_ref[0], sh_ref[0]).astype(BF16)
    qk_ref[0] = _dot(h, w_ref[:, 0:2 * dk])
    v_ref[0] = _dot(h, w_ref[:, 2 * dk:2 * dk + dv]).astype(BF16)
    r_ref[0] = _dot(h, w_ref[:, 2 * dk + dv:2 * dk + 2 * dv]).astype(BF16)
    g_low = _dot(h, w_ref[:, 2 * dk + 2 * dv:]).astype(BF16)
    gate = _dot(g_low, wg_ref[...]) + bg_ref[...]
    la_ref[0] = _log_sigmoid(gate) * (1.0 / GLA_GATE_TAU)


def gla_in_proj(x, gain, sc, sh, w_pad, wg_pad, b_gate, *, dk, dv, tm=512):
    b, t, d = x.shape
    n_pad = w_pad.shape[1]
    body = functools.partial(_gla_in_body, dk=dk, dv=dv)
    return pl.pallas_call(
        body,
        out_shape=(
            jax.ShapeDtypeStruct((b, t, 2 * dk), F32),
            jax.ShapeDtypeStruct((b, t, dv), BF16),
            jax.ShapeDtypeStruct((b, t, dv), BF16),
            jax.ShapeDtypeStruct((b, t, dk), F32),
        ),
        grid=(b, t // tm),
        in_specs=[
            pl.BlockSpec((1, tm, d), lambda i, j: (i, j, 0)),
            pl.BlockSpec((1, d), lambda i, j: (0, 0)),
            pl.BlockSpec((1, 1, d), lambda i, j: (i, 0, 0)),
            pl.BlockSpec((1, 1, d), lambda i, j: (i, 0, 0)),
            pl.BlockSpec((d, n_pad), lambda i, j: (0, 0)),
            pl.BlockSpec((LANES, dk), lambda i, j: (0, 0)),
            pl.BlockSpec((1, dk), lambda i, j: (0, 0)),
        ],
        out_specs=(
            pl.BlockSpec((1, tm, 2 * dk), lambda i, j: (i, j, 0)),
            pl.BlockSpec((1, tm, dv), lambda i, j: (i, j, 0)),
            pl.BlockSpec((1, tm, dv), lambda i, j: (i, j, 0)),
            pl.BlockSpec((1, tm, dk), lambda i, j: (i, j, 0)),
        ),
        compiler_params=_cparams(("parallel", "parallel")),
        name="gla_in_proj",
    )(x, gain, sc, sh, w_pad, wg_pad, b_gate)


def _gla_body(q_ref, k_ref, v_ref, la_ref, r_ref, on_ref, tri_ref, o_ref, st_ref, *, hk, n_chunks):
    @pl.when(pl.program_id(2) == 0)
    def _():
        st_ref[...] = jnp.zeros_like(st_ref)

    c = GLA_CHUNK
    tri = tri_ref[...]
    row = lax.broadcasted_iota(jnp.int32, (c, c), 0)
    col = lax.broadcasted_iota(jnp.int32, (c, c), 1)
    causal = row >= col
    for ci in range(n_chunks):
        sl = slice(ci * c, (ci + 1) * c)
        bcum = _exact_left_dot(tri, la_ref[0, sl, :])
        b_last = bcum[c - 1:c, :]
        q = q_ref[0, sl, :] * (hk ** -0.5)
        k = k_ref[0, sl, :]
        v = v_ref[0, sl, :]
        q_dec = (q * jnp.exp(bcum)).astype(BF16)
        k_inv = (k * jnp.exp(-bcum)).astype(BF16)
        k_end = (k * jnp.exp(b_last - bcum)).astype(BF16)
        attn = jnp.where(causal, _dot_nt(q_dec, k_inv), 0.0).astype(BF16)
        st = st_ref[...]
        o = _dot(attn, v) + _dot_nt(q_dec, st.astype(BF16))
        st_ref[...] = jnp.exp(b_last) * st + _dot_tn(v, k_end)
        o_n = o * lax.rsqrt(jnp.mean(o * o, axis=-1, keepdims=True) + EPS) * on_ref[...]
        o_ref[0, sl, :] = (o_n * _silu(r_ref[0, sl, :].astype(F32))).astype(BF16)


def gla_mix(qk, v, r, la, o_norm, *, heads, tt=512):
    b, t, dk2 = qk.shape
    dk = dk2 // 2
    dv = v.shape[-1]
    hk, hv = dk // heads, dv // heads
    tri = jnp.tri(GLA_CHUNK, dtype=BF16)
    body = functools.partial(_gla_body, hk=hk, n_chunks=tt // GLA_CHUNK)
    return pl.pallas_call(
        body,
        out_shape=jax.ShapeDtypeStruct((b, t, dv), BF16),
        grid=(b, heads, t // tt),
        in_specs=[
            pl.BlockSpec((1, tt, hk), lambda i, h, j: (i, j, h)),
            pl.BlockSpec((1, tt, hk), lambda i, h, j: (i, j, heads + h)),
            pl.BlockSpec((1, tt, hv), lambda i, h, j: (i, j, h)),
            pl.BlockSpec((1, tt, hk), lambda i, h, j: (i, j, h)),
            pl.BlockSpec((1, tt, hv), lambda i, h, j: (i, j, h)),
            pl.BlockSpec((1, hv), lambda i, h, j: (0, 0)),
            pl.BlockSpec((GLA_CHUNK, GLA_CHUNK), lambda i, h, j: (0, 0)),
        ],
        out_specs=pl.BlockSpec((1, tt, hv), lambda i, h, j: (i, j, h)),
        scratch_shapes=[pltpu.VMEM((hv, hk), F32)],
        compiler_params=_cparams(("parallel", "parallel", "arbitrary")),
        name="gla_mix",
    )(qk, qk, v, la, r, o_norm, tri)


def _out_proj_body(a_ref, w_ref, x_ref, g_ref, o_ref):
    o_ref[0] = x_ref[0] + g_ref[0] * _dot(a_ref[0], w_ref[...])


def out_proj_residual(a, w, x, g, tm=1024):
    b, t, d = x.shape
    k = a.shape[-1]
    return pl.pallas_call(
        _out_proj_body,
        out_shape=jax.ShapeDtypeStruct((b, t, d), F32),
        grid=(b, t // tm),
        in_specs=[
            pl.BlockSpec((1, tm, k), lambda i, j: (i, j, 0)),
            pl.BlockSpec((k, d), lambda i, j: (0, 0)),
            pl.BlockSpec((1, tm, d), lambda i, j: (i, j, 0)),
            pl.BlockSpec((1, 1, d), lambda i, j: (i, 0, 0)),
        ],
        out_specs=pl.BlockSpec((1, tm, d), lambda i, j: (i, j, 0)),
        compiler_params=_cparams(("parallel", "parallel")),
        name="out_proj_residual",
    )(a, w, x, g)


def _ffn_body(x_ref, gain_ref, sc_ref, sh_ref, g_ref, wg_ref, wu_ref, wd_ref, o_ref, h_ref, acc_ref):
    f = pl.program_id(2)

    @pl.when(f == 0)
    def _():
        h_ref[...] = _norm_mod(x_ref[0], gain_ref[...], sc_ref[0], sh_ref[0]).astype(BF16)
        acc_ref[...] = jnp.zeros_like(acc_ref)

    h = h_ref[...]
    a = (_silu(_dot(h, wg_ref[...])) * _dot(h, wu_ref[...])).astype(BF16)
    acc_ref[...] += _dot(a, wd_ref[...])

    @pl.when(f == pl.num_programs(2) - 1)
    def _():
        o_ref[0] = x_ref[0] + g_ref[0] * acc_ref[...]


def ffn_dense(x, gain, sc, sh, g, w_gate, w_up, w_down, tm=1024, tf=512):
    b, t, d = x.shape
    ff = w_gate.shape[1]
    return pl.pallas_call(
        _ffn_body,
        out_shape=jax.ShapeDtypeStruct((b, t, d), F32),
        grid=(b, t // tm, ff // tf),
        in_specs=[
            pl.BlockSpec((1, tm, d), lambda i, j, f: (i, j, 0)),
            pl.BlockSpec((1, d), lambda i, j, f: (0, 0)),
            pl.BlockSpec((1, 1, d), lambda i, j, f: (i, 0, 0)),
            pl.BlockSpec((1, 1, d), lambda i, j, f: (i, 0, 0)),
            pl.BlockSpec((1, 1, d), lambda i, j, f: (i, 0, 0)),
            pl.BlockSpec((d, tf), lambda i, j, f: (0, f)),
            pl.BlockSpec((d, tf), lambda i, j, f: (0, f)),
            pl.BlockSpec((tf, d), lambda i, j, f: (f, 0)),
        ],
        out_specs=pl.BlockSpec((1, tm, d), lambda i, j, f: (i, j, 0)),
        scratch_shapes=[pltpu.VMEM((tm, d), BF16), pltpu.VMEM((tm, d), F32)],
        compiler_params=_cparams(("parallel", "parallel", "arbitrary")),
        name="ffn_dense",
    )(x, gain, sc, sh, g, w_gate, w_up, w_down)


def _group_rms(x, gmean):
    sq = x * x
    hi = sq.astype(BF16)
    lo = (sq - hi.astype(F32)).astype(BF16)
    cols = gmean.shape[0]
    parts = []
    for s in range(0, x.shape[-1], cols):
        parts.append(_dot(hi[:, s:s + cols], gmean) + _dot(lo[:, s:s + cols], gmean))
    ms = jnp.concatenate(parts, axis=-1)
    return x * lax.rsqrt(ms + EPS)


def _fox_in_body(x_ref, gain_ref, sc_ref, sh_ref, w_ref, bf_ref, qn_ref, kn_ref, gm_ref, tri_ref,
                 q_ref, k_ref, v_ref, sg_ref, f_ref, carry_ref, *, d):
    @pl.when(pl.program_id(1) == 0)
    def _():
        carry_ref[...] = jnp.zeros_like(carry_ref)

    h = _norm_mod(x_ref[0], gain_ref[...], sc_ref[0], sh_ref[0]).astype(BF16)
    gm = gm_ref[...]
    q = _group_rms(_dot(h, w_ref[:, 0:d]), gm) * qn_ref[...]
    q_ref[0] = (q * (FOX_HEAD_DIM ** -0.5)).astype(BF16)
    k = _group_rms(_dot(h, w_ref[:, d:2 * d]), gm) * kn_ref[...]
    k_ref[0] = k.astype(BF16)
    v_ref[0] = _dot(h, w_ref[:, 2 * d:3 * d]).astype(BF16)
    sg_ref[0] = _sigmoid(_dot(h, w_ref[:, 3 * d:4 * d])).astype(BF16)
    log_f = _log_sigmoid(_dot(h, w_ref[:, 4 * d:]) + bf_ref[...])
    fcum = _exact_left_dot(tri_ref[...], log_f) + carry_ref[...]
    f_ref[0] = fcum
    carry_ref[...] = fcum[fcum.shape[0] - 1:, :]


def fox_in_proj(x, gain, sc, sh, w_pad, bf_pad, qn_t, kn_t, tm=512):
    b, t, d = x.shape
    n_pad = w_pad.shape[1]
    gcols = 2 * LANES
    gm = (jnp.kron(jnp.eye(gcols // FOX_HEAD_DIM, dtype=F32),
                   jnp.ones((FOX_HEAD_DIM, FOX_HEAD_DIM), F32)) / FOX_HEAD_DIM).astype(BF16)
    tri = jnp.tri(tm, dtype=BF16)
    body = functools.partial(_fox_in_body, d=d)
    bf = lambda s: jax.ShapeDtypeStruct(s, BF16)
    row = lambda i, j: (i, j, 0)
    const = lambda i, j: (0, 0)
    return pl.pallas_call(
        body,
        out_shape=(bf((b, t, d)), bf((b, t, d)), bf((b, t, d)), bf((b, t, d)),
                   jax.ShapeDtypeStruct((b, t, LANES), F32)),
        grid=(b, t // tm),
        in_specs=[
            pl.BlockSpec((1, tm, d), row),
            pl.BlockSpec((1, d), const),
            pl.BlockSpec((1, 1, d), lambda i, j: (i, 0, 0)),
            pl.BlockSpec((1, 1, d), lambda i, j: (i, 0, 0)),
            pl.BlockSpec((d, n_pad), const),
            pl.BlockSpec((1, LANES), const),
            pl.BlockSpec((1, d), const),
            pl.BlockSpec((1, d), const),
            pl.BlockSpec((gcols, gcols), const),
            pl.BlockSpec((tm, tm), const),
        ],
        out_specs=(
            pl.BlockSpec((1, tm, d), row), pl.BlockSpec((1, tm, d), row),
            pl.BlockSpec((1, tm, d), row), pl.BlockSpec((1, tm, d), row),
            pl.BlockSpec((1, tm, LANES), row),
        ),
        scratch_shapes=[pltpu.VMEM((1, LANES), F32)],
        compiler_params=_cparams(("parallel", "arbitrary")),
        name="fox_in_proj",
    )(x, gain, sc, sh, w_pad, bf_pad, qn_t, kn_t, gm, tri)


def _fox_attn_body(q_ref, k_ref, v_ref, sg_ref, fq_ref, fk_ref, o_ref, m_ref, l_ref, acc_ref, *, tq, tk):
    i = pl.program_id(2)
    j = pl.program_id(3)
    hd = FOX_HEAD_DIM
    lane = lax.broadcasted_iota(jnp.int32, (1, 2 * hd), 1)

    @pl.when(j == 0)
    def _():
        m_ref[...] = jnp.full_like(m_ref, -jnp.inf)
        l_ref[...] = jnp.zeros_like(l_ref)
        acc_ref[...] = jnp.zeros_like(acc_ref)

    def step(diagonal):
        q2 = q_ref[0]
        k2 = k_ref[0]
        v2 = v_ref[0]
        for hh in range(2):
            head_lanes = (lane >= hh * hd) & (lane < (hh + 1) * hd)
            qh = jnp.where(head_lanes, q2, jnp.zeros_like(q2))
            s = _dot_nt(qh, k2) + fq_ref[0, hh] - fk_ref[0, hh]
            if diagonal:
                row = lax.broadcasted_iota(jnp.int32, (tq, tk), 0)
                col = lax.broadcasted_iota(jnp.int32, (tq, tk), 1)
                s = jnp.where(row >= col, s, -jnp.inf)
            m_prev = m_ref[hh]
            m_new = jnp.maximum(m_prev, jnp.max(s, axis=-1, keepdims=True))
            alpha = jnp.exp(m_prev - m_new)
            p = jnp.exp(s - m_new)
            l_ref[hh] = alpha * l_ref[hh] + jnp.sum(p, axis=-1, keepdims=True)
            acc_ref[hh] = alpha * acc_ref[hh] + _dot(p.astype(BF16), v2)
            m_ref[hh] = m_new

    @pl.when(j < i)
    def _():
        step(False)

    @pl.when(j == i)
    def _():
        step(True)
        o0 = acc_ref[0] / l_ref[0]
        o1 = acc_ref[1] / l_ref[1]
        o = jnp.where(lane < hd, o0, o1)
        o_ref[0] = (o * sg_ref[0].astype(F32)).astype(BF16)


def fox_attention(q, k, v, sg, f_col, f_row, tq=512):
    b, t, d = q.shape
    pairs = d // (2 * FOX_HEAD_DIM)
    tk = tq
    nq = t // tq
    body = functools.partial(_fox_attn_body, tq=tq, tk=tk)
    qmap = lambda bi, p, i, j: (bi, i, p)
    kmap = lambda bi, p, i, j: (bi, jnp.minimum(j, i), p)
    return pl.pallas_call(
        body,
        out_shape=jax.ShapeDtypeStruct((b, t, d), BF16),
        grid=(b, pairs, nq, nq),
        in_specs=[
            pl.BlockSpec((1, tq, 2 * FOX_HEAD_DIM), qmap),
            pl.BlockSpec((1, tk, 2 * FOX_HEAD_DIM), kmap),
            pl.BlockSpec((1, tk, 2 * FOX_HEAD_DIM), kmap),
            pl.BlockSpec((1, tq, 2 * FOX_HEAD_DIM), qmap),
            pl.BlockSpec((1, 2, tq, 1), lambda bi, p, i, j: (bi, p, i, 0)),
            pl.BlockSpec((1, 2, 1, tk), lambda bi, p, i, j: (bi, p, 0, jnp.minimum(j, i))),
        ],
        out_specs=pl.BlockSpec((1, tq, 2 * FOX_HEAD_DIM), qmap),
        scratch_shapes=[
            pltpu.VMEM((2, tq, 1), F32),
            pltpu.VMEM((2, tq, 1), F32),
            pltpu.VMEM((2, tq, 2 * FOX_HEAD_DIM), F32),
        ],
        compiler_params=_cparams(("parallel", "parallel", "parallel", "arbitrary")),
        name="fox_attention",
    )(q, k, v, sg, f_col, f_row)


def _router_body(x_ref, gain_ref, sc_ref, sh_ref, wr_ref, br_ref, h_ref, comb_ref):
    h = _norm_mod(x_ref[0], gain_ref[...], sc_ref[0], sh_ref[0])
    h_ref[0] = h.astype(BF16)
    logits = jnp.dot(h, wr_ref[...], preferred_element_type=F32,
                     precision=lax.Precision.HIGHEST) + br_ref[...]
    lane = lax.broadcasted_iota(jnp.int32, logits.shape, 1)
    logits = jnp.where(lane < N_EXPERTS, logits, -jnp.inf)
    m1 = jnp.max(logits, axis=-1, keepdims=True)
    i1 = jnp.min(jnp.where(logits == m1, lane, LANES), axis=-1, keepdims=True)
    rest = jnp.where(lane == i1, -jnp.inf, logits)
    m2 = jnp.max(rest, axis=-1, keepdims=True)
    i2 = jnp.min(jnp.where(rest == m2, lane, LANES), axis=-1, keepdims=True)
    e2 = jnp.exp(m2 - m1)
    w1 = 1.0 / (1.0 + e2)
    w2 = e2 / (1.0 + e2)
    comb_ref[0] = jnp.where(lane == i1, w1, 0.0) + jnp.where(lane == i2, w2, 0.0)


def moe_router(x, gain, sc, sh, wr_pad, br_pad, tm=512):
    b, t, d = x.shape
    row = lambda i, j: (i, j, 0)
    const = lambda i, j: (0, 0)
    return pl.pallas_call(
        _router_body,
        out_shape=(jax.ShapeDtypeStruct((b, t, d), BF16), jax.ShapeDtypeStruct((b, t, LANES), F32)),
        grid=(b, t // tm),
        in_specs=[
            pl.BlockSpec((1, tm, d), row),
            pl.BlockSpec((1, d), const),
            pl.BlockSpec((1, 1, d), lambda i, j: (i, 0, 0)),
            pl.BlockSpec((1, 1, d), lambda i, j: (i, 0, 0)),
            pl.BlockSpec((d, LANES), const),
            pl.BlockSpec((1, LANES), const),
        ],
        out_specs=(pl.BlockSpec((1, tm, d), row), pl.BlockSpec((1, tm, LANES), row)),
        compiler_params=_cparams(("parallel", "parallel")),
        name="moe_router",
    )(x, gain, sc, sh, wr_pad, br_pad)


def _moe_body(h_ref, comb_ref, x_ref, g_ref, wg_ref, wu_ref, wd_ref, o_ref, acc_ref):
    e = pl.program_id(2)
    f = pl.program_id(3)

    @pl.when((e == 0) & (f == 0))
    def _():
        acc_ref[...] = jnp.zeros_like(acc_ref)

    h = h_ref[0]
    comb = comb_ref[0]
    lane = lax.broadcasted_iota(jnp.int32, comb.shape, 1)
    ce = jnp.sum(jnp.where(lane == e, comb, 0.0), axis=-1, keepdims=True)
    a = (_silu(_dot(h, wg_ref[0])) * _dot(h, wu_ref[0])).astype(BF16)
    acc_ref[...] += ce * _dot(a, wd_ref[0])

    @pl.when((e == pl.num_programs(2) - 1) & (f == pl.num_programs(3) - 1))
    def _():
        o_ref[0] = x_ref[0] + g_ref[0] * acc_ref[...]


def moe_experts(h, comb, x, g, w_gate, w_up, w_down, tm=1024, tf=512):
    b, t, d = x.shape
    ne, _, ff = w_gate.shape
    row = lambda i, j, e, f: (i, j, 0)
    return pl.pallas_call(
        _moe_body,
        out_shape=jax.ShapeDtypeStruct((b, t, d), F32),
        grid=(b, t // tm, ne, ff // tf),
        in_specs=[
            pl.BlockSpec((1, tm, d), row),
            pl.BlockSpec((1, tm, LANES), row),
            pl.BlockSpec((1, tm, d), row),
            pl.BlockSpec((1, 1, d), lambda i, j, e, f: (i, 0, 0)),
            pl.BlockSpec((1, d, tf), lambda i, j, e, f: (e, 0, f)),
            pl.BlockSpec((1, d, tf), lambda i, j, e, f: (e, 0, f)),
            pl.BlockSpec((1, tf, d), lambda i, j, e, f: (e, f, 0)),
        ],
        out_specs=pl.BlockSpec((1, tm, d), row),
        scratch_shapes=[pltpu.VMEM((tm, d), F32)],
        compiler_params=_cparams(("parallel", "parallel", "arbitrary", "arbitrary")),
        name="moe_experts",
    )(h, comb, x, g, w_gate, w_up, w_down)


def _pad_cols(w, n):
    return jnp.pad(w, ((0, 0), (0, n - w.shape[1])))


def kernel(x, c, ada_w, ada_b, norm_gain, gla_w_in, gla_w_gate_up, gla_b_gate, gla_o_norm, gla_w_out,
           fox_w_in, fox_b_f, fox_q_norm, fox_k_norm, fox_w_out, ffn_w_gate, ffn_w_up, ffn_w_down,
           moe_w_router, moe_b_router, moe_w_gate, moe_w_up, moe_w_down):
    b, t, d = x.shape
    depth = ada_w.shape[0]
    gla_rank = gla_w_gate_up.shape[1]
    dk = gla_w_gate_up.shape[2]
    dv = gla_w_out.shape[1]
    fox_heads = fox_b_f.shape[1]

    c_pad = jnp.pad(c, ((0, 8 - b), (0, 0)))
    mods = ada_mods(c_pad, ada_w, ada_b)[:, :b]
    mods = mods.reshape(depth, b, 6, 1, d)

    for i in range(depth):
        j = i // 2
        sh1, sc1, g1, sh2, sc2, g2 = (mods[i, :, m] for m in range(6))
        gain1 = norm_gain[i, 0].reshape(1, d)
        gain2 = norm_gain[i, 1].reshape(1, d)
        if i % 2 == 0:
            n_main = 2 * dk + 2 * dv
            w_pad = _pad_cols(gla_w_in[j], n_main + LANES).astype(BF16)
            wg_pad = jnp.pad(gla_w_gate_up[j], ((0, LANES - gla_rank), (0, 0))).astype(BF16)
            qk, v, r, la = gla_in_proj(x, gain1, sc1, sh1, w_pad, wg_pad, gla_b_gate[j].reshape(1, dk),
                                       dk=dk, dv=dv)
            a = gla_mix(qk, v, r, la, gla_o_norm[j].reshape(1, -1), heads=GLA_HEADS)
            x = out_proj_residual(a, gla_w_out[j].astype(BF16), x, g1)
            x = ffn_dense(x, gain2, sc2, sh2, g2, ffn_w_gate[j].astype(BF16), ffn_w_up[j].astype(BF16),
                          ffn_w_down[j].astype(BF16))
        else:
            w_pad = _pad_cols(fox_w_in[j], 4 * d + LANES).astype(BF16)
            bf_pad = jnp.pad(fox_b_f[j], (0, LANES - fox_heads)).reshape(1, LANES)
            qn_t = jnp.tile(fox_q_norm[j], fox_heads).reshape(1, d)
            kn_t = jnp.tile(fox_k_norm[j], fox_heads).reshape(1, d)
            q, k, v, sg, fcum = fox_in_proj(x, gain1, sc1, sh1, w_pad, bf_pad, qn_t, kn_t)
            f_bht = jnp.transpose(fcum[:, :, :fox_heads], (0, 2, 1))
            a = fox_attention(q, k, v, sg, f_bht[..., None], f_bht[:, :, None, :])
            x = out_proj_residual(a, fox_w_out[j].astype(BF16), x, g1)
            wr_pad = _pad_cols(moe_w_router[j], LANES)
            br_pad = jnp.pad(moe_b_router[j], (0, LANES - N_EXPERTS)).reshape(1, LANES)
            h, comb = moe_router(x, gain2, sc2, sh2, wr_pad, br_pad)
            x = moe_experts(h, comb, x, g2, moe_w_gate[j].astype(BF16), moe_w_up[j].astype(BF16),
                            moe_w_down[j].astype(BF16))
    return x
```

```python
import functools

import jax
import jax.numpy as jnp
from jax import lax
from jax.experimental import pallas as pl
from jax.experimental.pallas import tpu as pltpu

F32 = jnp.float32
BF16 = jnp.bfloat16

EPS = 1e-6
GLA_HEADS = 4
GLA_GATE_TAU = 16.0
GLA_CHUNK = 64
FOX_HEAD_DIM = 64
LOG2E = 1.4426950408889634
N_EXPERTS = 8
MOE_ROW_TILE = 1024
MOE_TOKEN_TILE = 512
LANES = 128
VMEM_LIMIT = 56 * 1024 * 1024


def _cparams(sem):
    return pltpu.CompilerParams(dimension_semantics=sem, vmem_limit_bytes=VMEM_LIMIT)


def _dot(a, b):
    return jnp.dot(a, b, preferred_element_type=F32)


def _dot_nt(a, b):
    return lax.dot_general(a, b, (((1,), (1,)), ((), ())), preferred_element_type=F32)


def _dot_tn(a, b):
    return lax.dot_general(a, b, (((0,), (0,)), ((), ())), preferred_element_type=F32)


def _split3(a):
    a1 = a.astype(BF16)
    r1 = a - a1.astype(F32)
    a2 = r1.astype(BF16)
    a3 = (r1 - a2.astype(F32)).astype(BF16)
    return a1, a2, a3


def _exact_left_dot(m01, a):
    a1, a2, a3 = _split3(a)
    return _dot(m01, a1) + _dot(m01, a2) + _dot(m01, a3)


def _log_sigmoid(x):
    return jnp.minimum(x, 0.0) - jnp.log1p(jnp.exp(-jnp.abs(x)))


def _sigmoid(x):
    return 1.0 / (1.0 + jnp.exp(-x))


def _silu(x):
    return x * _sigmoid(x)


def _norm_mod(x, gain, scale, shift):
    y = x * lax.rsqrt(jnp.mean(x * x, axis=-1, keepdims=True) + EPS)
    return (y * gain) * (1.0 + scale) + shift


def _ada_body(c_ref, w_ref, b_ref, o_ref):
    cond = _silu(c_ref[...]).astype(BF16)
    o_ref[0] = _dot(cond, w_ref[0].astype(BF16)) + b_ref[0]


def ada_mods(c_pad, ada_w, ada_b, tn=1536):
    depth, d, n = ada_w.shape
    rows = c_pad.shape[0]
    return pl.pallas_call(
        _ada_body,
        out_shape=jax.ShapeDtypeStruct((depth, rows, n), F32),
        grid=(depth, n // tn),
        in_specs=[
            pl.BlockSpec((rows, d), lambda i, j: (0, 0)),
            pl.BlockSpec((1, d, tn), lambda i, j: (i, 0, j)),
            pl.BlockSpec((1, 1, tn), lambda i, j: (i, 0, j)),
        ],
        out_specs=pl.BlockSpec((1, rows, tn), lambda i, j: (i, 0, j)),
        compiler_params=_cparams(("parallel", "parallel")),
        name="ada_mods",
    )(c_pad, ada_w, ada_b.reshape(depth, 1, n))


def _gla_in_body(x_ref, gain_ref, sc_ref, sh_ref, w_ref, wg_ref, bg_ref,
                 qk_ref, v_ref, r_ref, la_ref, *, dk, dv):
    h = _norm_mod(x_ref[0], gain_ref[...], sc_ref[0], sh_ref[0]).astype(BF16)
    qk_ref[0] = _dot(h, w_ref[:, 0:2 * dk])
    v_ref[0] = _dot(h, w_ref[:, 2 * dk:2 * dk + dv]).astype(BF16)
    r_ref[0] = _dot(h, w_ref[:, 2 * dk + dv:2 * dk + 2 * dv]).astype(BF16)
    g_low = _dot(h, w_ref[:, 2 * dk + 2 * dv:]).astype(BF16)
    gate = _dot(g_low, wg_ref[...]) + bg_ref[...]
    la_ref[0] = _log_sigmoid(gate) * (1.0 / GLA_GATE_TAU)


def gla_in_proj(x, gain, sc, sh, w_pad, wg_pad, b_gate, *, dk, dv, tm=512):
    b, t, d = x.shape
    n_pad = w_pad.shape[1]
    body = functools.partial(_gla_in_body, dk=dk, dv=dv)
    return pl.pallas_call(
        body,
        out_shape=(
            jax.ShapeDtypeStruct((b, t, 2 * dk), F32),
            jax.ShapeDtypeStruct((b, t, dv), BF16),
            jax.ShapeDtypeStruct((b, t, dv), BF16),
            jax.ShapeDtypeStruct((b, t, dk), F32),
        ),
        grid=(b, t // tm),
        in_specs=[
            pl.BlockSpec((1, tm, d), lambda i, j: (i, j, 0)),
            pl.BlockSpec((1, d), lambda i, j: (0, 0)),
            pl.BlockSpec((1, 1, d), lambda i, j: (i, 0, 0)),
            pl.BlockSpec((1, 1, d), lambda i, j: (i, 0, 0)),
            pl.BlockSpec((d, n_pad), lambda i, j: (0, 0)),
            pl.BlockSpec((LANES, dk), lambda i, j: (0, 0)),
            pl.BlockSpec((1, dk), lambda i, j: (0, 0)),
        ],
        out_specs=(
            pl.BlockSpec((1, tm, 2 * dk), lambda i, j: (i, j, 0)),
            pl.BlockSpec((1, tm, dv), lambda i, j: (i, j, 0)),
            pl.BlockSpec((1, tm, dv), lambda i, j: (i, j, 0)),
            pl.BlockSpec((1, tm, dk), lambda i, j: (i, j, 0)),
        ),
        compiler_params=_cparams(("parallel", "parallel")),
        name="gla_in_proj",
    )(x, gain, sc, sh, w_pad, wg_pad, b_gate)


def _gla_body(q_ref, k_ref, v_ref, la_ref, r_ref, on_ref, tri_ref, o_ref, st_ref, *, hk, n_chunks):
    @pl.when(pl.program_id(2) == 0)
    def _():
        st_ref[...] = jnp.zeros_like(st_ref)

    c = GLA_CHUNK
    tri = tri_ref[...]
    row = lax.broadcasted_iota(jnp.int32, (c, c), 0)
    col = lax.broadcasted_iota(jnp.int32, (c, c), 1)
    causal = row >= col
    for ci in range(n_chunks):
        sl = slice(ci * c, (ci + 1) * c)
        bcum = _exact_left_dot(tri, la_ref[0, sl, :])
        b_last = bcum[c - 1:c, :]
        q = q_ref[0, sl, :] * (hk ** -0.5)
        k = k_ref[0, sl, :]
        v = v_ref[0, sl, :]
        q_dec = (q * jnp.exp(bcum)).astype(BF16)
        k_inv = (k * jnp.exp(-bcum)).astype(BF16)
        k_end = (k * jnp.exp(b_last - bcum)).astype(BF16)
        attn = jnp.where(causal, _dot_nt(q_dec, k_inv), 0.0).astype(BF16)
        st = st_ref[...]
        o = _dot(attn, v) + _dot_nt(q_dec, st.astype(BF16))
        st_ref[...] = jnp.exp(b_last) * st + _dot_tn(v, k_end)
        o_n = o * lax.rsqrt(jnp.mean(o * o, axis=-1, keepdims=True) + EPS) * on_ref[...]
        o_ref[0, sl, :] = (o_n * _silu(r_ref[0, sl, :].astype(F32))).astype(BF16)


def gla_mix(qk, v, r, la, o_norm, *, heads, tt=512):
    b, t, dk2 = qk.shape
    dk = dk2 // 2
    dv = v.shape[-1]
    hk, hv = dk // heads, dv // heads
    tri = jnp.tri(GLA_CHUNK, dtype=BF16)
    body = functools.partial(_gla_body, hk=hk, n_chunks=tt // GLA_CHUNK)
    return pl.pallas_call(
        body,
        out_shape=jax.ShapeDtypeStruct((b, t, dv), BF16),
        grid=(b, heads, t // tt),
        in_specs=[
            pl.BlockSpec((1, tt, hk), lambda i, h, j: (i, j, h)),
            pl.BlockSpec((1, tt, hk), lambda i, h, j: (i, j, heads + h)),
            pl.BlockSpec((1, tt, hv), lambda i, h, j: (i, j, h)),
            pl.BlockSpec((1, tt, hk), lambda i, h, j: (i, j, h)),
            pl.BlockSpec((1, tt, hv), lambda i, h, j: (i, j, h)),
            pl.BlockSpec((1, hv), lambda i, h, j: (0, 0)),
            pl.BlockSpec((GLA_CHUNK, GLA_CHUNK), lambda i, h, j: (0, 0)),
        ],
        out_specs=pl.BlockSpec((1, tt, hv), lambda i, h, j: (i, j, h)),
        scratch_shapes=[pltpu.VMEM((hv, hk), F32)],
        compiler_params=_cparams(("parallel", "parallel", "arbitrary")),
        name="gla_mix",
    )(qk, qk, v, la, r, o_norm, tri)


def _out_proj_body(a_ref, w_ref, x_ref, g_ref, o_ref):
    o_ref[0] = x_ref[0] + g_ref[0] * _dot(a_ref[0], w_ref[...])


def out_proj_residual(a, w, x, g, tm=1024):
    b, t, d = x.shape
    k = a.shape[-1]
    return pl.pallas_call(
        _out_proj_body,
        out_shape=jax.ShapeDtypeStruct((b, t, d), F32),
        grid=(b, t // tm),
        in_specs=[
            pl.BlockSpec((1, tm, k), lambda i, j: (i, j, 0)),
            pl.BlockSpec((k, d), lambda i, j: (0, 0)),
            pl.BlockSpec((1, tm, d), lambda i, j: (i, j, 0)),
            pl.BlockSpec((1, 1, d), lambda i, j: (i, 0, 0)),
        ],
        out_specs=pl.BlockSpec((1, tm, d), lambda i, j: (i, j, 0)),
        compiler_params=_cparams(("parallel", "parallel")),
        name="out_proj_residual",
    )(a, w, x, g)


def _ffn_body(x_ref, gain_ref, sc_ref, sh_ref, g_ref, wg_ref, wu_ref, wd_ref, o_ref, h_ref, acc_ref):
    f = pl.program_id(2)

    @pl.when(f == 0)
    def _():
        h_ref[...] = _norm_mod(x_ref[0], gain_ref[...], sc_ref[0], sh_ref[0]).astype(BF16)
        acc_ref[...] = jnp.zeros_like(acc_ref)

    h = h_ref[...]
    a = (_silu(_dot(h, wg_ref[...])) * _dot(h, wu_ref[...])).astype(BF16)
    acc_ref[...] += _dot(a, wd_ref[...])

    @pl.when(f == pl.num_programs(2) - 1)
    def _():
        o_ref[0] = x_ref[0] + g_ref[0] * acc_ref[...]


def ffn_dense(x, gain, sc, sh, g, w_gate, w_up, w_down, tm=1024, tf=512):
    b, t, d = x.shape
    ff = w_gate.shape[1]
    return pl.pallas_call(
        _ffn_body,
        out_shape=jax.ShapeDtypeStruct((b, t, d), F32),
        grid=(b, t // tm, ff // tf),
        in_specs=[
            pl.BlockSpec((1, tm, d), lambda i, j, f: (i, j, 0)),
            pl.BlockSpec((1, d), lambda i, j, f: (0, 0)),
            pl.BlockSpec((1, 1, d), lambda i, j, f: (i, 0, 0)),
            pl.BlockSpec((1, 1, d), lambda i, j, f: (i, 0, 0)),
            pl.BlockSpec((1, 1, d), lambda i, j, f: (i, 0, 0)),
            pl.BlockSpec((d, tf), lambda i, j, f: (0, f)),
            pl.BlockSpec((d, tf), lambda i, j, f: (0, f)),
            pl.BlockSpec((tf, d), lambda i, j, f: (f, 0)),
        ],
        out_specs=pl.BlockSpec((1, tm, d), lambda i, j, f: (i, j, 0)),
        scratch_shapes=[pltpu.VMEM((tm, d), BF16), pltpu.VMEM((tm, d), F32)],
        compiler_params=_cparams(("parallel", "parallel", "arbitrary")),
        name="ffn_dense",
    )(x, gain, sc, sh, g, w_gate, w_up, w_down)


def _group_rms(x, gmean):
    sq = x * x
    hi = sq.astype(BF16)
    lo = (sq - hi.astype(F32)).astype(BF16)
    cols = gmean.shape[0]
    parts = []
    for s in range(0, x.shape[-1], cols):
        parts.append(_dot(hi[:, s:s + cols], gmean) + _dot(lo[:, s:s + cols], gmean))
    ms = jnp.concatenate(parts, axis=-1)
    return x * lax.rsqrt(ms + EPS)


def _fox_in_body(x_ref, gain_ref, sc_ref, sh_ref, w_ref, bf_ref, qn_ref, kn_ref, gm_ref, tri_ref, pf_ref,
                 q_ref, k_ref, kf_ref, v_ref, sg_ref, carry_ref, *, d, heads):
    @pl.when(pl.program_id(1) == 0)
    def _():
        carry_ref[...] = jnp.zeros_like(carry_ref)

    h = _norm_mod(x_ref[0], gain_ref[...], sc_ref[0], sh_ref[0]).astype(BF16)
    gm = gm_ref[...]
    q = _group_rms(_dot(h, w_ref[:, 0:d]), gm) * qn_ref[...]
    q_ref[0] = (q * (FOX_HEAD_DIM ** -0.5 * LOG2E)).astype(BF16)
    k = _group_rms(_dot(h, w_ref[:, d:2 * d]), gm) * kn_ref[...]
    k_ref[0] = k.astype(BF16)
    v_ref[0] = _dot(h, w_ref[:, 2 * d:3 * d]).astype(BF16)
    sg_ref[0] = _sigmoid(_dot(h, w_ref[:, 3 * d:4 * d])).astype(BF16)
    log_f = _log_sigmoid(_dot(h, w_ref[:, 4 * d:]) + bf_ref[...])
    fcum = _exact_left_dot(tri_ref[...], log_f) + carry_ref[...]
    carry_ref[...] = fcum[fcum.shape[0] - 1:, :]
    lane = lax.broadcasted_iota(jnp.int32, fcum.shape, 1)
    p1, p2, p3 = _split3(jnp.where(lane < heads, fcum * (-LOG2E), 0.0))
    packed = (p1.astype(F32) + pltpu.roll(p2.astype(F32), heads, 1)
              + pltpu.roll(p3.astype(F32), 2 * heads, 1)).astype(BF16)
    kf_ref[0] = _dot(packed, pf_ref[...]).astype(BF16)


def _fox_bias_placement(heads, d):
    src, dst = [], []
    for i in range(3):
        for hd in range(heads):
            src.append(i * heads + hd)
            dst.append((hd // 2) * LANES + 3 * (hd % 2) + i)
    return jnp.zeros((LANES, d), F32).at[jnp.array(src), jnp.array(dst)].set(1.0).astype(BF16)


def fox_in_proj(x, gain, sc, sh, w_pad, bf_pad, qn_t, kn_t, heads, tm=512):
    b, t, d = x.shape
    n_pad = w_pad.shape[1]
    gcols = 2 * LANES
    gm = (jnp.kron(jnp.eye(gcols // FOX_HEAD_DIM, dtype=F32),
                   jnp.ones((FOX_HEAD_DIM, FOX_HEAD_DIM), F32)) / FOX_HEAD_DIM).astype(BF16)
    tri = jnp.tri(tm, dtype=BF16)
    pf = _fox_bias_placement(heads, d)
    body = functools.partial(_fox_in_body, d=d, heads=heads)
    bf = lambda s: jax.ShapeDtypeStruct(s, BF16)
    row = lambda i, j: (i, j, 0)
    const = lambda i, j: (0, 0)
    return pl.pallas_call(
        body,
        out_shape=tuple(bf((b, t, d)) for _ in range(5)),
        grid=(b, t // tm),
        in_specs=[
            pl.BlockSpec((1, tm, d), row),
            pl.BlockSpec((1, d), const),
            pl.BlockSpec((1, 1, d), lambda i, j: (i, 0, 0)),
            pl.BlockSpec((1, 1, d), lambda i, j: (i, 0, 0)),
            pl.BlockSpec((d, n_pad), const),
            pl.BlockSpec((1, LANES), const),
            pl.BlockSpec((1, d), const),
            pl.BlockSpec((1, d), const),
            pl.BlockSpec((gcols, gcols), const),
            pl.BlockSpec((tm, tm), const),
            pl.BlockSpec((LANES, d), const),
        ],
        out_specs=tuple(pl.BlockSpec((1, tm, d), row) for _ in range(5)),
        scratch_shapes=[pltpu.VMEM((1, LANES), F32)],
        compiler_params=_cparams(("parallel", "arbitrary")),
        name="fox_in_proj",
    )(x, gain, sc, sh, w_pad, bf_pad, qn_t, kn_t, gm, tri, pf)


def _fox_attn_body(q_ref, k_ref, kf_ref, vt_ref, sg_ref, o_ref, m_ref, l_ref, acc_ref, *, tq, tk):
    i = pl.program_id(2)
    hd = FOX_HEAD_DIM
    lane = lax.broadcasted_iota(jnp.int32, (1, LANES), 1)
    q2 = q_ref[0]
    qs = []
    for hh in range(2):
        qh = jnp.where((lane >= hh * hd) & (lane < (hh + 1) * hd), q2, jnp.zeros_like(q2))
        ones = jnp.where((lane >= 3 * hh) & (lane < 3 * hh + 3), 1.0, 0.0).astype(BF16)
        qs.append(jnp.concatenate([qh, jnp.broadcast_to(ones, qh.shape)], axis=-1))

    m_ref[...] = jnp.full_like(m_ref, -jnp.inf)
    l_ref[...] = jnp.zeros_like(l_ref)
    acc_ref[...] = jnp.zeros_like(acc_ref)

    def kv_step(j, diagonal):
        ks = pl.multiple_of(j * tk, tk)
        kcat = jnp.concatenate([k_ref[0, pl.ds(ks, tk), :], kf_ref[0, pl.ds(ks, tk), :]], axis=-1)
        vt = vt_ref[0, 0, j]
        for hh in range(2):
            st = _dot_nt(kcat, qs[hh])
            if diagonal:
                key = lax.broadcasted_iota(jnp.int32, (tk, tq), 0)
                qry = lax.broadcasted_iota(jnp.int32, (tk, tq), 1)
                st = jnp.where(key <= qry, st, -jnp.inf)
            m_prev = m_ref[hh]
            m_new = jnp.maximum(m_prev, jnp.max(st, axis=0, keepdims=True))
            alpha = jnp.exp2(m_prev - m_new)
            p = jnp.exp2(st - m_new)
            l_ref[hh] = alpha * l_ref[hh] + jnp.sum(p, axis=0, keepdims=True)
            acc_ref[hh] = alpha * acc_ref[hh] + _dot(vt, p.astype(BF16))
            m_ref[hh] = m_new

    def off_diagonal(j, carry):
        kv_step(j, False)
        return carry

    lax.fori_loop(0, i, off_diagonal, 0)
    kv_step(i, True)

    chan = lax.broadcasted_iota(jnp.int32, (2 * hd, 1), 0)
    o_t = jnp.where(chan < hd, acc_ref[0] * (1.0 / l_ref[0]), acc_ref[1] * (1.0 / l_ref[1]))
    o_ref[0] = (o_t.T * sg_ref[0].astype(F32)).astype(BF16)


def fox_attention(q, k, kf, v, sg, tq=512):
    b, t, d = q.shape
    pw = 2 * FOX_HEAD_DIM
    pairs = d // pw
    tk = tq
    nk = t // tk
    vt = jnp.transpose(v.reshape(b, nk, tk, pairs, pw), (0, 3, 1, 4, 2))
    body = functools.partial(_fox_attn_body, tq=tq, tk=tk)
    qmap = lambda bi, p, i: (bi, i, p)
    kmap = lambda bi, p, i: (bi, 0, p)
    return pl.pallas_call(
        body,
        out_shape=jax.ShapeDtypeStruct((b, t, d), BF16),
        grid=(b, pairs, t // tq),
        in_specs=[
            pl.BlockSpec((1, tq, pw), qmap),
            pl.BlockSpec((1, t, pw), kmap),
            pl.BlockSpec((1, t, pw), kmap),
            pl.BlockSpec((1, 1, nk, pw, tk), lambda bi, p, i: (bi, p, 0, 0, 0)),
            pl.BlockSpec((1, tq, pw), qmap),
        ],
        out_specs=pl.BlockSpec((1, tq, pw), qmap),
        scratch_shapes=[
            pltpu.VMEM((2, 1, tq), F32),
            pltpu.VMEM((2, 1, tq), F32),
            pltpu.VMEM((2, pw, tq), F32),
        ],
        compiler_params=_cparams(("parallel", "parallel", "arbitrary")),
        name="fox_attention",
    )(q, k, kf, vt, sg)


META_E1, META_E2, META_W1, META_W2, META_R1, META_R2 = range(6)


def _lane_pick(a, lane, idx):
    return jnp.sum(jnp.where(lane == idx, a, 0.0), axis=-1, keepdims=True)


def _router_body(x_ref, gain_ref, sc_ref, sh_ref, wr_ref, br_ref, tri_ref, hp_ref, meta_ref, cnt_ref, carry_ref):
    @pl.when((pl.program_id(0) == 0) & (pl.program_id(1) == 0))
    def _():
        carry_ref[...] = jnp.zeros_like(carry_ref)

    h = _norm_mod(x_ref[0], gain_ref[...], sc_ref[0], sh_ref[0])
    half = h.shape[-1] // 2
    hb = h.astype(BF16).astype(F32)
    lo = lax.bitcast_convert_type(hb[:, :half], jnp.uint32) >> 16
    hi = lax.bitcast_convert_type(hb[:, half:], jnp.uint32) & jnp.uint32(0xFFFF0000)
    hp_ref[0] = hi | lo

    logits = jnp.dot(h, wr_ref[...], preferred_element_type=F32,
                     precision=lax.Precision.HIGHEST) + br_ref[...]
    lane = lax.broadcasted_iota(jnp.int32, logits.shape, 1)
    logits = jnp.where(lane < N_EXPERTS, logits, -jnp.inf)
    m1 = jnp.max(logits, axis=-1, keepdims=True)
    i1 = jnp.min(jnp.where(logits == m1, lane, LANES), axis=-1, keepdims=True)
    rest = jnp.where(lane == i1, -jnp.inf, logits)
    m2 = jnp.max(rest, axis=-1, keepdims=True)
    i2 = jnp.min(jnp.where(rest == m2, lane, LANES), axis=-1, keepdims=True)
    e2 = jnp.exp(m2 - m1)
    w1 = 1.0 / (1.0 + e2)
    w2 = e2 / (1.0 + e2)

    onehot = jnp.where(lane == i1, 1.0, 0.0) + jnp.where(lane == i2, 1.0, 0.0)
    before = _dot(tri_ref[...], onehot.astype(BF16)) + carry_ref[...]
    r1 = _lane_pick(before, lane, i1)
    r2 = _lane_pick(before, lane, i2)
    last = onehot.shape[0] - 1
    total = before[last:, :] + onehot[last:, :]
    carry_ref[...] = total
    cnt_ref[...] = total

    meta = jnp.zeros_like(logits)
    for slot, val in ((META_E1, i1.astype(F32)), (META_E2, i2.astype(F32)), (META_W1, w1), (META_W2, w2),
                      (META_R1, r1), (META_R2, r2)):
        meta = jnp.where(lane == slot, val, meta)
    meta_ref[0] = meta


def moe_router(x, gain, sc, sh, wr_pad, br_pad, tm=512):
    b, t, d = x.shape
    row = lambda i, j: (i, j, 0)
    const = lambda i, j: (0, 0)
    tri = jnp.tri(tm, k=-1, dtype=BF16)
    return pl.pallas_call(
        _router_body,
        out_shape=(jax.ShapeDtypeStruct((b, t, d // 2), jnp.uint32),
                   jax.ShapeDtypeStruct((b, t, LANES), F32),
                   jax.ShapeDtypeStruct((1, LANES), F32)),
        grid=(b, t // tm),
        in_specs=[
            pl.BlockSpec((1, tm, d), row),
            pl.BlockSpec((1, d), const),
            pl.BlockSpec((1, 1, d), lambda i, j: (i, 0, 0)),
            pl.BlockSpec((1, 1, d), lambda i, j: (i, 0, 0)),
            pl.BlockSpec((d, LANES), const),
            pl.BlockSpec((1, LANES), const),
            pl.BlockSpec((tm, tm), const),
        ],
        out_specs=(pl.BlockSpec((1, tm, d // 2), row), pl.BlockSpec((1, tm, LANES), row),
                   pl.BlockSpec((1, LANES), const)),
        scratch_shapes=[pltpu.VMEM((1, LANES), F32)],
        compiler_params=_cparams(("arbitrary", "arbitrary")),
        name="moe_router",
    )(x, gain, sc, sh, wr_pad, br_pad, tri)


def _row_copy(src_ref, src_row, dst_ref, dst_row, sem):
    return pltpu.make_async_copy(src_ref.at[pl.ds(src_row, 1), :], dst_ref.at[pl.ds(dst_row, 1), :], sem)


def _dispatch_body(dest_ref, hp_ref, xs_in_ref, xs_ref, sem, *, tm):
    del xs_in_ref

    def issue(r, carry):
        _row_copy(hp_ref, r, xs_ref, dest_ref[0, 0, r], sem).start()
        _row_copy(hp_ref, r, xs_ref, dest_ref[0, 0, tm + r], sem).start()
        return carry

    lax.fori_loop(0, tm, issue, 0, unroll=8)

    def drain(r, carry):
        _row_copy(hp_ref, r, xs_ref, dest_ref[0, 0, r], sem).wait()
        _row_copy(hp_ref, r, xs_ref, dest_ref[0, 0, tm + r], sem).wait()
        return carry

    lax.fori_loop(0, tm, drain, 0, unroll=8)


def moe_dispatch(hp, dest, rows, tm):
    n, half = hp.shape
    body = functools.partial(_dispatch_body, tm=tm)
    return pl.pallas_call(
        body,
        out_shape=jax.ShapeDtypeStruct((rows, half), jnp.uint32),
        grid=(n // tm,),
        in_specs=[
            pl.BlockSpec((1, 1, 2 * tm), lambda i: (i, 0, 0), memory_space=pltpu.SMEM),
            pl.BlockSpec((tm, half), lambda i: (i, 0)),
            pl.BlockSpec(memory_space=pl.ANY),
        ],
        out_specs=pl.BlockSpec(memory_space=pl.ANY),
        scratch_shapes=[pltpu.SemaphoreType.DMA],
        input_output_aliases={2: 0},
        compiler_params=pltpu.CompilerParams(dimension_semantics=("arbitrary",), vmem_limit_bytes=VMEM_LIMIT,
                                             disable_bounds_checks=True),
        name="moe_dispatch",
    )(dest, hp, jnp.zeros((rows, half), jnp.uint32))


def _experts_body(te_ref, nu_ref, xs_ref, wg_ref, wu_ref, wd_ref, y_ref, h_ref, acc_ref):
    del te_ref
    i = pl.program_id(0)
    f = pl.program_id(1)

    @pl.when(i < nu_ref[0])
    def _():
        @pl.when(f == 0)
        def _():
            xp = xs_ref[...]
            lo = lax.bitcast_convert_type(xp << 16, F32).astype(BF16)
            hi = lax.bitcast_convert_type(xp & jnp.uint32(0xFFFF0000), F32).astype(BF16)
            h_ref[...] = jnp.concatenate([lo, hi], axis=-1)
            acc_ref[...] = jnp.zeros_like(acc_ref)

        h = h_ref[...]
        a = (_silu(_dot(h, wg_ref[0].astype(BF16))) * _dot(h, wu_ref[0].astype(BF16))).astype(BF16)
        acc_ref[...] += _dot(a, wd_ref[0].astype(BF16))

        @pl.when(f == pl.num_programs(1) - 1)
        def _():
            y_ref[...] = acc_ref[...]

    @pl.when((i >= nu_ref[0]) & (f == pl.num_programs(1) - 1))
    def _():
        y_ref[...] = jnp.zeros_like(y_ref)


def moe_experts(xs, tile_expert, n_used, w_gate, w_up, w_down, tr, tf=512):
    rows, half = xs.shape
    ne, d, ff = w_gate.shape
    nf = ff // tf
    tile = lambda i, nu: jnp.minimum(i, nu[0] - 1)
    fblk = lambda i, f, nu: jnp.where(i < nu[0], f, nf - 1)
    return pl.pallas_call(
        _experts_body,
        out_shape=jax.ShapeDtypeStruct((rows, d), F32),
        grid_spec=pltpu.PrefetchScalarGridSpec(
            num_scalar_prefetch=2,
            grid=(rows // tr, nf),
            in_specs=[
                pl.BlockSpec((tr, half), lambda i, f, te, nu: (tile(i, nu), 0)),
                pl.BlockSpec((1, d, tf), lambda i, f, te, nu: (te[tile(i, nu)], 0, fblk(i, f, nu))),
                pl.BlockSpec((1, d, tf), lambda i, f, te, nu: (te[tile(i, nu)], 0, fblk(i, f, nu))),
                pl.BlockSpec((1, tf, d), lambda i, f, te, nu: (te[tile(i, nu)], fblk(i, f, nu), 0)),
            ],
            out_specs=pl.BlockSpec((tr, d), lambda i, f, te, nu: (i, 0)),
            scratch_shapes=[pltpu.VMEM((tr, d), BF16), pltpu.VMEM((tr, d), F32)],
        ),
        compiler_params=_cparams(("arbitrary", "arbitrary")),
        name="moe_experts",
    )(tile_expert, n_used, xs, w_gate, w_up, w_down)


def _combine_body(dest_ref, x_ref, g_ref, meta_ref, y_ref, o_ref, buf_ref, sem, *, tm):
    def issue(r, carry):
        _row_copy(y_ref, dest_ref[0, 0, r], buf_ref.at[0], r, sem).start()
        _row_copy(y_ref, dest_ref[0, 0, tm + r], buf_ref.at[1], r, sem).start()
        return carry

    lax.fori_loop(0, tm, issue, 0, unroll=8)

    def drain(r, carry):
        _row_copy(y_ref, dest_ref[0, 0, r], buf_ref.at[0], r, sem).wait()
        _row_copy(y_ref, dest_ref[0, 0, tm + r], buf_ref.at[1], r, sem).wait()
        return carry

    lax.fori_loop(0, tm, drain, 0, unroll=8)

    meta = meta_ref[...]
    lane = lax.broadcasted_iota(jnp.int32, meta.shape, 1)
    w1 = _lane_pick(meta, lane, META_W1)
    w2 = _lane_pick(meta, lane, META_W2)
    o_ref[...] = x_ref[...] + g_ref[0] * (w1 * buf_ref[0] + w2 * buf_ref[1])


def moe_combine(x2, g, meta2, y, dest, t, tm):
    n, d = x2.shape
    per_batch = t // tm
    body = functools.partial(_combine_body, tm=tm)
    return pl.pallas_call(
        body,
        out_shape=jax.ShapeDtypeStruct((n, d), F32),
        grid=(n // tm,),
        in_specs=[
            pl.BlockSpec((1, 1, 2 * tm), lambda i: (i, 0, 0), memory_space=pltpu.SMEM),
            pl.BlockSpec((tm, d), lambda i: (i, 0)),
            pl.BlockSpec((1, 1, d), lambda i: (i // per_batch, 0, 0)),
            pl.BlockSpec((tm, LANES), lambda i: (i, 0)),
            pl.BlockSpec(memory_space=pl.ANY),
        ],
        out_specs=pl.BlockSpec((tm, d), lambda i: (i, 0)),
        scratch_shapes=[pltpu.VMEM((2, tm, d), F32), pltpu.SemaphoreType.DMA],
        compiler_params=pltpu.CompilerParams(dimension_semantics=("arbitrary",), vmem_limit_bytes=VMEM_LIMIT,
                                             disable_bounds_checks=True),
        name="moe_combine",
    )(dest, x2, g, meta2, y)


def moe_plan(meta2, counts, tr, tm):
    n = meta2.shape[0]
    e1 = meta2[:, META_E1].astype(jnp.int32)
    e2 = meta2[:, META_E2].astype(jnp.int32)
    cnt = counts[0, :N_EXPERTS].astype(jnp.int32)
    padded = (cnt + tr - 1) // tr * tr
    ends = jnp.cumsum(padded)
    starts = ends - padded
    d1 = starts[e1] + meta2[:, META_R1].astype(jnp.int32)
    d2 = starts[e2] + meta2[:, META_R2].astype(jnp.int32)
    dest = jnp.concatenate([d1.reshape(n // tm, 1, tm), d2.reshape(n // tm, 1, tm)], axis=-1)
    n_tiles = 2 * n // tr + N_EXPERTS
    tile_start = jnp.arange(n_tiles, dtype=jnp.int32) * tr
    tile_expert = jnp.minimum(jnp.searchsorted(ends, tile_start, side="right"), N_EXPERTS - 1).astype(jnp.int32)
    n_used = (ends[-1:] // tr).astype(jnp.int32)
    return dest, tile_expert, n_used, n_tiles * tr


def _pad_cols(w, n):
    return jnp.pad(w, ((0, 0), (0, n - w.shape[1])))


def kernel(x, c, ada_w, ada_b, norm_gain, gla_w_in, gla_w_gate_up, gla_b_gate, gla_o_norm, gla_w_out,
           fox_w_in, fox_b_f, fox_q_norm, fox_k_norm, fox_w_out, ffn_w_gate, ffn_w_up, ffn_w_down,
           moe_w_router, moe_b_router, moe_w_gate, moe_w_up, moe_w_down):
    b, t, d = x.shape
    depth = ada_w.shape[0]
    gla_rank = gla_w_gate_up.shape[1]
    dk = gla_w_gate_up.shape[2]
    dv = gla_w_out.shape[1]
    fox_heads = fox_b_f.shape[1]

    c_pad = jnp.pad(c, ((0, 8 - b), (0, 0)))
    mods = ada_mods(c_pad, ada_w, ada_b)[:, :b]
    mods = mods.reshape(depth, b, 6, 1, d)

    for i in range(depth):
        j = i // 2
        sh1, sc1, g1, sh2, sc2, g2 = (mods[i, :, m] for m in range(6))
        gain1 = norm_gain[i, 0].reshape(1, d)
        gain2 = norm_gain[i, 1].reshape(1, d)
        if i % 2 == 0:
            n_main = 2 * dk + 2 * dv
            w_pad = _pad_cols(gla_w_in[j], n_main + LANES).astype(BF16)
            wg_pad = jnp.pad(gla_w_gate_up[j], ((0, LANES - gla_rank), (0, 0))).astype(BF16)
            qk, v, r, la = gla_in_proj(x, gain1, sc1, sh1, w_pad, wg_pad, gla_b_gate[j].reshape(1, dk),
                                       dk=dk, dv=dv)
            a = gla_mix(qk, v, r, la, gla_o_norm[j].reshape(1, -1), heads=GLA_HEADS)
            x = out_proj_residual(a, gla_w_out[j].astype(BF16), x, g1)
            x = ffn_dense(x, gain2, sc2, sh2, g2, ffn_w_gate[j].astype(BF16), ffn_w_up[j].astype(BF16),
                          ffn_w_down[j].astype(BF16))
        else:
            w_pad = _pad_cols(fox_w_in[j], 4 * d + LANES).astype(BF16)
            bf_pad = jnp.pad(fox_b_f[j], (0, LANES - fox_heads)).reshape(1, LANES)
            qn_t = jnp.tile(fox_q_norm[j], fox_heads).reshape(1, d)
            kn_t = jnp.tile(fox_k_norm[j], fox_heads).reshape(1, d)
            q, k, kf, v, sg = fox_in_proj(x, gain1, sc1, sh1, w_pad, bf_pad, qn_t, kn_t, fox_heads)
            a = fox_attention(q, k, kf, v, sg)
            x = out_proj_residual(a, fox_w_out[j].astype(BF16), x, g1)
            wr_pad = _pad_cols(moe_w_router[j], LANES)
            br_pad = jnp.pad(moe_b_router[j], (0, LANES - N_EXPERTS)).reshape(1, LANES)
            hp, meta, counts = moe_router(x, gain2, sc2, sh2, wr_pad, br_pad)
            meta2 = meta.reshape(b * t, LANES)
            dest, tile_expert, n_used, rows = moe_plan(meta2, counts, MOE_ROW_TILE, MOE_TOKEN_TILE)
            xs = moe_dispatch(hp.reshape(b * t, d // 2), dest, rows, MOE_TOKEN_TILE)
            y = moe_experts(xs, tile_expert, n_used, moe_w_gate[j], moe_w_up[j], moe_w_down[j], MOE_ROW_TILE)
            x = moe_combine(x.reshape(b * t, d), g2, meta2, y, dest, t, MOE_TOKEN_TILE).reshape(b, t, d)
    return x
```

```python
import functools

import jax
import jax.numpy as jnp
from jax import lax
from jax.experimental import pallas as pl
from jax.experimental.pallas import tpu as pltpu

F32 = jnp.float32
BF16 = jnp.bfloat16

EPS = 1e-6
GLA_HEADS = 4
GLA_GATE_TAU = 16.0
GLA_CHUNK = 64
FOX_HEAD_DIM = 64
LOG2E = 1.4426950408889634
N_EXPERTS = 8
MOE_ROW_TILE = 1024
MOE_TOKEN_TILE = 512
LANES = 128
VMEM_LIMIT = 56 * 1024 * 1024


def _cparams(sem):
    return pltpu.CompilerParams(dimension_semantics=sem, vmem_limit_bytes=VMEM_LIMIT)


def _dot(a, b):
    return jnp.dot(a, b, preferred_element_type=F32)


def _dot_nt(a, b):
    return lax.dot_general(a, b, (((1,), (1,)), ((), ())), preferred_element_type=F32)


def _dot_tn(a, b):
    return lax.dot_general(a, b, (((0,), (0,)), ((), ())), preferred_element_type=F32)


def _split3(a):
    a1 = a.astype(BF16)
    r1 = a - a1.astype(F32)
    a2 = r1.astype(BF16)
    a3 = (r1 - a2.astype(F32)).astype(BF16)
    return a1, a2, a3


def _exact_left_dot(m01, a):
    a1, a2, a3 = _split3(a)
    return _dot(m01, a1) + _dot(m01, a2) + _dot(m01, a3)


def _log_sigmoid(x):
    return jnp.minimum(x, 0.0) - jnp.log1p(jnp.exp(-jnp.abs(x)))


def _sigmoid(x):
    return 1.0 / (1.0 + jnp.exp(-x))


def _silu(x):
    return x * _sigmoid(x)


def _norm_mod(x, gain, scale, shift):
    y = x * lax.rsqrt(jnp.mean(x * x, axis=-1, keepdims=True) + EPS)
    return (y * gain) * (1.0 + scale) + shift


def _ada_body(c_ref, w_ref, b_ref, o_ref):
    cond = _silu(c_ref[...]).astype(BF16)
    o_ref[0] = _dot(cond, w_ref[0].astype(BF16)) + b_ref[0]


def ada_mods(c_pad, ada_w, ada_b, tn=1536):
    depth, d, n = ada_w.shape
    rows = c_pad.shape[0]
    return pl.pallas_call(
        _ada_body,
        out_shape=jax.ShapeDtypeStruct((depth, rows, n), F32),
        grid=(depth, n // tn),
        in_specs=[
            pl.BlockSpec((rows, d), lambda i, j: (0, 0)),
            pl.BlockSpec((1, d, tn), lambda i, j: (i, 0, j)),
            pl.BlockSpec((1, 1, tn), lambda i, j: (i, 0, j)),
        ],
        out_specs=pl.BlockSpec((1, rows, tn), lambda i, j: (i, 0, j)),
        compiler_params=_cparams(("parallel", "parallel")),
        name="ada_mods",
    )(c_pad, ada_w, ada_b.reshape(depth, 1, n))


def _gla_in_body(x_ref, gain_ref, sc_ref, sh_ref, w_ref, wg_ref, bg_ref,
                 qk_ref, v_ref, r_ref, la_ref, *, dk, dv):
    h = _norm_mod(x_ref[0], gain_ref[...], sc_ref[0], sh_ref[0]).astype(BF16)
    qk_ref[0] = _dot(h, w_ref[:, 0:2 * dk])
    v_ref[0] = _dot(h, w_ref[:, 2 * dk:2 * dk + dv]).astype(BF16)
    r_ref[0] = _dot(h, w_ref[:, 2 * dk + dv:2 * dk + 2 * dv]).astype(BF16)
    g_low = _dot(h, w_ref[:, 2 * dk + 2 * dv:]).astype(BF16)
    gate = _dot(g_low, wg_ref[...]) + bg_ref[...]
    la_ref[0] = _log_sigmoid(gate) * (1.0 / GLA_GATE_TAU)


def gla_in_proj(x, gain, sc, sh, w_pad, wg_pad, b_gate, *, dk, dv, tm=512):
    b, t, d = x.shape
    n_pad = w_pad.shape[1]
    body = functools.partial(_gla_in_body, dk=dk, dv=dv)
    return pl.pallas_call(
        body,
        out_shape=(
            jax.ShapeDtypeStruct((b, t, 2 * dk), F32),
            jax.ShapeDtypeStruct((b, t, dv), BF16),
            jax.ShapeDtypeStruct((b, t, dv), BF16),
            jax.ShapeDtypeStruct((b, t, dk), F32),
        ),
        grid=(b, t // tm),
        in_specs=[
            pl.BlockSpec((1, tm, d), lambda i, j: (i, j, 0)),
            pl.BlockSpec((1, d), lambda i, j: (0, 0)),
            pl.BlockSpec((1, 1, d), lambda i, j: (i, 0, 0)),
            pl.BlockSpec((1, 1, d), lambda i, j: (i, 0, 0)),
            pl.BlockSpec((d, n_pad), lambda i, j: (0, 0)),
            pl.BlockSpec((LANES, dk), lambda i, j: (0, 0)),
            pl.BlockSpec((1, dk), lambda i, j: (0, 0)),
        ],
        out_specs=(
            pl.BlockSpec((1, tm, 2 * dk), lambda i, j: (i, j, 0)),
            pl.BlockSpec((1, tm, dv), lambda i, j: (i, j, 0)),
            pl.BlockSpec((1, tm, dv), lambda i, j: (i, j, 0)),
            pl.BlockSpec((1, tm, dk), lambda i, j: (i, j, 0)),
        ),
        compiler_params=_cparams(("parallel", "parallel")),
        name="gla_in_proj",
    )(x, gain, sc, sh, w_pad, wg_pad, b_gate)


def _gla_body(qk_ref, v_ref, la_ref, r_ref, on_ref, tri_ref, o_ref, st_ref, *, heads, hk, hv, n_chunks):
    @pl.when(pl.program_id(1) == 0)
    def _():
        st_ref[...] = jnp.zeros_like(st_ref)

    c = GLA_CHUNK
    dk = heads * hk
    tri = tri_ref[...]
    row = lax.broadcasted_iota(jnp.int32, (c, c), 0)
    col = lax.broadcasted_iota(jnp.int32, (c, c), 1)
    causal = row >= col
    for ci in range(n_chunks):
        sl = slice(ci * c, (ci + 1) * c)
        for hh in range(heads):
            ksl = slice(hh * hk, (hh + 1) * hk)
            vsl = slice(hh * hv, (hh + 1) * hv)
            bcum = _exact_left_dot(tri, la_ref[0, sl, ksl])
            b_last = bcum[c - 1:c, :]
            q = qk_ref[0, sl, ksl] * (hk ** -0.5)
            k = qk_ref[0, sl, dk + hh * hk:dk + (hh + 1) * hk]
            v = v_ref[0, sl, vsl]
            q_dec = (q * jnp.exp(bcum)).astype(BF16)
            k_inv = (k * jnp.exp(-bcum)).astype(BF16)
            k_end = (k * jnp.exp(b_last - bcum)).astype(BF16)
            attn = jnp.where(causal, _dot_nt(q_dec, k_inv), 0.0).astype(BF16)
            st = st_ref[hh]
            o = _dot(attn, v) + _dot_nt(q_dec, st.astype(BF16))
            st_ref[hh] = jnp.exp(b_last) * st + _dot_tn(v, k_end)
            o_n = o * lax.rsqrt(jnp.mean(o * o, axis=-1, keepdims=True) + EPS) * on_ref[...]
            o_ref[0, sl, vsl] = (o_n * _silu(r_ref[0, sl, vsl].astype(F32))).astype(BF16)


def gla_mix(qk, v, r, la, o_norm, *, heads, tt=256):
    b, t, dk2 = qk.shape
    dk = dk2 // 2
    dv = v.shape[-1]
    hk, hv = dk // heads, dv // heads
    tri = jnp.tri(GLA_CHUNK, dtype=BF16)
    body = functools.partial(_gla_body, heads=heads, hk=hk, hv=hv, n_chunks=tt // GLA_CHUNK)
    row = lambda i, j: (i, j, 0)
    return pl.pallas_call(
        body,
        out_shape=jax.ShapeDtypeStruct((b, t, dv), BF16),
        grid=(b, t // tt),
        in_specs=[
            pl.BlockSpec((1, tt, dk2), row),
            pl.BlockSpec((1, tt, dv), row),
            pl.BlockSpec((1, tt, dk), row),
            pl.BlockSpec((1, tt, dv), row),
            pl.BlockSpec((1, hv), lambda i, j: (0, 0)),
            pl.BlockSpec((GLA_CHUNK, GLA_CHUNK), lambda i, j: (0, 0)),
        ],
        out_specs=pl.BlockSpec((1, tt, dv), row),
        scratch_shapes=[pltpu.VMEM((heads, hv, hk), F32)],
        compiler_params=_cparams(("parallel", "arbitrary")),
        name="gla_mix",
    )(qk, v, la, r, o_norm, tri)


def _out_proj_body(a_ref, w_ref, x_ref, g_ref, o_ref):
    o_ref[0] = x_ref[0] + g_ref[0] * _dot(a_ref[0], w_ref[...])


def out_proj_residual(a, w, x, g, tm=1024):
    b, t, d = x.shape
    k = a.shape[-1]
    return pl.pallas_call(
        _out_proj_body,
        out_shape=jax.ShapeDtypeStruct((b, t, d), F32),
        grid=(b, t // tm),
        in_specs=[
            pl.BlockSpec((1, tm, k), lambda i, j: (i, j, 0)),
            pl.BlockSpec((k, d), lambda i, j: (0, 0)),
            pl.BlockSpec((1, tm, d), lambda i, j: (i, j, 0)),
            pl.BlockSpec((1, 1, d), lambda i, j: (i, 0, 0)),
        ],
        out_specs=pl.BlockSpec((1, tm, d), lambda i, j: (i, j, 0)),
        compiler_params=_cparams(("parallel", "parallel")),
        name="out_proj_residual",
    )(a, w, x, g)


def _ffn_body(x_ref, gain_ref, sc_ref, sh_ref, g_ref, wg_ref, wu_ref, wd_ref, o_ref, h_ref, acc_ref):
    f = pl.program_id(2)

    @pl.when(f == 0)
    def _():
        h_ref[...] = _norm_mod(x_ref[0], gain_ref[...], sc_ref[0], sh_ref[0]).astype(BF16)
        acc_ref[...] = jnp.zeros_like(acc_ref)

    h = h_ref[...]
    a = (_silu(_dot(h, wg_ref[...])) * _dot(h, wu_ref[...])).astype(BF16)
    acc_ref[...] += _dot(a, wd_ref[...])

    @pl.when(f == pl.num_programs(2) - 1)
    def _():
        o_ref[0] = x_ref[0] + g_ref[0] * acc_ref[...]


def ffn_dense(x, gain, sc, sh, g, w_gate, w_up, w_down, tm=1024, tf=512):
    b, t, d = x.shape
    ff = w_gate.shape[1]
    return pl.pallas_call(
        _ffn_body,
        out_shape=jax.ShapeDtypeStruct((b, t, d), F32),
        grid=(b, t // tm, ff // tf),
        in_specs=[
            pl.BlockSpec((1, tm, d), lambda i, j, f: (i, j, 0)),
            pl.BlockSpec((1, d), lambda i, j, f: (0, 0)),
            pl.BlockSpec((1, 1, d), lambda i, j, f: (i, 0, 0)),
            pl.BlockSpec((1, 1, d), lambda i, j, f: (i, 0, 0)),
            pl.BlockSpec((1, 1, d), lambda i, j, f: (i, 0, 0)),
            pl.BlockSpec((d, tf), lambda i, j, f: (0, f)),
            pl.BlockSpec((d, tf), lambda i, j, f: (0, f)),
            pl.BlockSpec((tf, d), lambda i, j, f: (f, 0)),
        ],
        out_specs=pl.BlockSpec((1, tm, d), lambda i, j, f: (i, j, 0)),
        scratch_shapes=[pltpu.VMEM((tm, d), BF16), pltpu.VMEM((tm, d), F32)],
        compiler_params=_cparams(("parallel", "parallel", "arbitrary")),
        name="ffn_dense",
    )(x, gain, sc, sh, g, w_gate, w_up, w_down)


def _group_rms(x, gmean):
    sq = x * x
    hi = sq.astype(BF16)
    lo = (sq - hi.astype(F32)).astype(BF16)
    cols = gmean.shape[0]
    parts = []
    for s in range(0, x.shape[-1], cols):
        parts.append(_dot(hi[:, s:s + cols], gmean) + _dot(lo[:, s:s + cols], gmean))
    ms = jnp.concatenate(parts, axis=-1)
    return x * lax.rsqrt(ms + EPS)


def _fox_in_body(x_ref, gain_ref, sc_ref, sh_ref, w_ref, bf_ref, qn_ref, kn_ref, gm_ref, tri_ref, pf_ref,
                 q_ref, k_ref, kf_ref, v_ref, sg_ref, carry_ref, *, d, heads):
    @pl.when(pl.program_id(1) == 0)
    def _():
        carry_ref[...] = jnp.zeros_like(carry_ref)

    h = _norm_mod(x_ref[0], gain_ref[...], sc_ref[0], sh_ref[0]).astype(BF16)
    gm = gm_ref[...]
    q = _group_rms(_dot(h, w_ref[:, 0:d]), gm) * qn_ref[...]
    q_ref[0] = (q * (FOX_HEAD_DIM ** -0.5 * LOG2E)).astype(BF16)
    k = _group_rms(_dot(h, w_ref[:, d:2 * d]), gm) * kn_ref[...]
    k_ref[0] = k.astype(BF16)
    v_ref[0] = _dot(h, w_ref[:, 2 * d:3 * d]).astype(BF16)
    sg_ref[0] = _sigmoid(_dot(h, w_ref[:, 3 * d:4 * d])).astype(BF16)
    log_f = _log_sigmoid(_dot(h, w_ref[:, 4 * d:]) + bf_ref[...])
    fcum = _exact_left_dot(tri_ref[...], log_f) + carry_ref[...]
    carry_ref[...] = fcum[fcum.shape[0] - 1:, :]
    lane = lax.broadcasted_iota(jnp.int32, fcum.shape, 1)
    p1, p2, p3 = _split3(jnp.where(lane < heads, fcum * (-LOG2E), 0.0))
    packed = (p1.astype(F32) + pltpu.roll(p2.astype(F32), heads, 1)
              + pltpu.roll(p3.astype(F32), 2 * heads, 1)).astype(BF16)
    kf_ref[0] = _dot(packed, pf_ref[...]).astype(BF16)


def _fox_bias_placement(heads, d):
    src, dst = [], []
    for i in range(3):
        for hd in range(heads):
            src.append(i * heads + hd)
            dst.append((hd // 2) * LANES + 3 * (hd % 2) + i)
    return jnp.zeros((LANES, d), F32).at[jnp.array(src), jnp.array(dst)].set(1.0).astype(BF16)


def fox_in_proj(x, gain, sc, sh, w_pad, bf_pad, qn_t, kn_t, heads, tm=512):
    b, t, d = x.shape
    n_pad = w_pad.shape[1]
    gcols = 2 * LANES
    gm = (jnp.kron(jnp.eye(gcols // FOX_HEAD_DIM, dtype=F32),
                   jnp.ones((FOX_HEAD_DIM, FOX_HEAD_DIM), F32)) / FOX_HEAD_DIM).astype(BF16)
    tri = jnp.tri(tm, dtype=BF16)
    pf = _fox_bias_placement(heads, d)
    body = functools.partial(_fox_in_body, d=d, heads=heads)
    bf = lambda s: jax.ShapeDtypeStruct(s, BF16)
    row = lambda i, j: (i, j, 0)
    const = lambda i, j: (0, 0)
    return pl.pallas_call(
        body,
        out_shape=tuple(bf((b, t, d)) for _ in range(5)),
        grid=(b, t // tm),
        in_specs=[
            pl.BlockSpec((1, tm, d), row),
            pl.BlockSpec((1, d), const),
            pl.BlockSpec((1, 1, d), lambda i, j: (i, 0, 0)),
            pl.BlockSpec((1, 1, d), lambda i, j: (i, 0, 0)),
            pl.BlockSpec((d, n_pad), const),
            pl.BlockSpec((1, LANES), const),
            pl.BlockSpec((1, d), const),
            pl.BlockSpec((1, d), const),
            pl.BlockSpec((gcols, gcols), const),
            pl.BlockSpec((tm, tm), const),
            pl.BlockSpec((LANES, d), const),
        ],
        out_specs=tuple(pl.BlockSpec((1, tm, d), row) for _ in range(5)),
        scratch_shapes=[pltpu.VMEM((1, LANES), F32)],
        compiler_params=_cparams(("parallel", "arbitrary")),
        name="fox_in_proj",
    )(x, gain, sc, sh, w_pad, bf_pad, qn_t, kn_t, gm, tri, pf)


def _fox_attn_body(q_ref, k_ref, kf_ref, vt_ref, sg_ref, o_ref, s0_ref, s1_ref, mx0_ref, mx1_ref,
                   m_ref, l_ref, acc_ref, *, tq, tk):
    i = pl.program_id(2)
    hd = FOX_HEAD_DIM
    lane = lax.broadcasted_iota(jnp.int32, (1, LANES), 1)
    q2 = q_ref[0]
    qs = []
    for hh in range(2):
        qh = jnp.where((lane >= hh * hd) & (lane < (hh + 1) * hd), q2, jnp.zeros_like(q2))
        ones = jnp.where((lane >= 3 * hh) & (lane < 3 * hh + 3), 1.0, 0.0).astype(BF16)
        qs.append(jnp.concatenate([qh, jnp.broadcast_to(ones, qh.shape)], axis=-1))

    m_ref[...] = jnp.full_like(m_ref, -jnp.inf)
    l_ref[...] = jnp.zeros_like(l_ref)
    acc_ref[...] = jnp.zeros_like(acc_ref)

    def scores(j, s_ref, mx_ref, diagonal):
        ks = pl.multiple_of(j * tk, tk)
        kcat = jnp.concatenate([k_ref[0, pl.ds(ks, tk), :], kf_ref[0, pl.ds(ks, tk), :]], axis=-1)
        for hh in range(2):
            st = _dot_nt(kcat, qs[hh])
            if diagonal:
                key = lax.broadcasted_iota(jnp.int32, (tk, tq), 0)
                qry = lax.broadcasted_iota(jnp.int32, (tk, tq), 1)
                st = jnp.where(key <= qry, st, -jnp.inf)
            s_ref[hh] = st
            mx_ref[hh] = jnp.max(st, axis=0, keepdims=True)

    def consume(j, s_ref, mx_ref):
        vt = vt_ref[0, 0, j]
        for hh in range(2):
            m_prev = m_ref[hh]
            m_new = jnp.maximum(m_prev, mx_ref[hh])
            alpha = jnp.exp2(m_prev - m_new)
            p = jnp.exp2(s_ref[hh] - m_new)
            l_ref[hh] = alpha * l_ref[hh] + jnp.sum(p, axis=0, keepdims=True)
            acc_ref[hh] = alpha * acc_ref[hh] + _dot(vt, p.astype(BF16))
            m_ref[hh] = m_new

    scores(i, s0_ref, mx0_ref, True)

    def pair(u, carry):
        t = 2 * u
        scores(t, s1_ref, mx1_ref, False)
        consume(jnp.where(u == 0, i, t - 1), s0_ref, mx0_ref)
        scores(t + 1, s0_ref, mx0_ref, False)
        consume(t, s1_ref, mx1_ref)
        return carry

    lax.fori_loop(0, i // 2, pair, 0)

    @pl.when(i % 2 == 1)
    def _():
        scores(i - 1, s1_ref, mx1_ref, False)
        consume(jnp.where(i == 1, i, i - 2), s0_ref, mx0_ref)
        consume(i - 1, s1_ref, mx1_ref)

    @pl.when(i % 2 == 0)
    def _():
        consume(jnp.maximum(i - 1, 0), s0_ref, mx0_ref)

    chan = lax.broadcasted_iota(jnp.int32, (2 * hd, 1), 0)
    o_t = jnp.where(chan < hd, acc_ref[0] * (1.0 / l_ref[0]), acc_ref[1] * (1.0 / l_ref[1]))
    o_ref[0] = (o_t.T * sg_ref[0].astype(F32)).astype(BF16)


def fox_attention(q, k, kf, v, sg, tq=512):
    b, t, d = q.shape
    pw = 2 * FOX_HEAD_DIM
    pairs = d // pw
    tk = tq
    nk = t // tk
    vt = jnp.transpose(v.reshape(b, nk, tk, pairs, pw), (0, 3, 1, 4, 2))
    body = functools.partial(_fox_attn_body, tq=tq, tk=tk)
    qmap = lambda bi, p, i: (bi, i, p)
    kmap = lambda bi, p, i: (bi, 0, p)
    return pl.pallas_call(
        body,
        out_shape=jax.ShapeDtypeStruct((b, t, d), BF16),
        grid=(b, pairs, t // tq),
        in_specs=[
            pl.BlockSpec((1, tq, pw), qmap),
            pl.BlockSpec((1, t, pw), kmap),
            pl.BlockSpec((1, t, pw), kmap),
            pl.BlockSpec((1, 1, nk, pw, tk), lambda bi, p, i: (bi, p, 0, 0, 0)),
            pl.BlockSpec((1, tq, pw), qmap),
        ],
        out_specs=pl.BlockSpec((1, tq, pw), qmap),
        scratch_shapes=[
            pltpu.VMEM((2, tk, tq), F32),
            pltpu.VMEM((2, tk, tq), F32),
            pltpu.VMEM((2, 1, tq), F32),
            pltpu.VMEM((2, 1, tq), F32),
            pltpu.VMEM((2, 1, tq), F32),
            pltpu.VMEM((2, 1, tq), F32),
            pltpu.VMEM((2, pw, tq), F32),
        ],
        compiler_params=_cparams(("parallel", "parallel", "arbitrary")),
        name="fox_attention",
    )(q, k, kf, vt, sg)


META_E1, META_E2, META_W1, META_W2, META_R1, META_R2 = range(6)


def _lane_pick(a, lane, idx):
    return jnp.sum(jnp.where(lane == idx, a, 0.0), axis=-1, keepdims=True)


def _router_body(x_ref, gain_ref, sc_ref, sh_ref, wr_ref, br_ref, tri_ref, hp_ref, meta_ref, cnt_ref, carry_ref):
    @pl.when((pl.program_id(0) == 0) & (pl.program_id(1) == 0))
    def _():
        carry_ref[...] = jnp.zeros_like(carry_ref)

    h = _norm_mod(x_ref[0], gain_ref[...], sc_ref[0], sh_ref[0])
    half = h.shape[-1] // 2
    hb = h.astype(BF16).astype(F32)
    lo = lax.bitcast_convert_type(hb[:, :half], jnp.uint32) >> 16
    hi = lax.bitcast_convert_type(hb[:, half:], jnp.uint32) & jnp.uint32(0xFFFF0000)
    hp_ref[0] = hi | lo

    logits = jnp.dot(h, wr_ref[...], preferred_element_type=F32,
                     precision=lax.Precision.HIGHEST) + br_ref[...]
    lane = lax.broadcasted_iota(jnp.int32, logits.shape, 1)
    logits = jnp.where(lane < N_EXPERTS, logits, -jnp.inf)
    m1 = jnp.max(logits, axis=-1, keepdims=True)
    i1 = jnp.min(jnp.where(logits == m1, lane, LANES), axis=-1, keepdims=True)
    rest = jnp.where(lane == i1, -jnp.inf, logits)
    m2 = jnp.max(rest, axis=-1, keepdims=True)
    i2 = jnp.min(jnp.where(rest == m2, lane, LANES), axis=-1, keepdims=True)
    e2 = jnp.exp(m2 - m1)
    w1 = 1.0 / (1.0 + e2)
    w2 = e2 / (1.0 + e2)

    onehot = jnp.where(lane == i1, 1.0, 0.0) + jnp.where(lane == i2, 1.0, 0.0)
    before = _dot(tri_ref[...], onehot.astype(BF16)) + carry_ref[...]
    r1 = _lane_pick(before, lane, i1)
    r2 = _lane_pick(before, lane, i2)
    last = onehot.shape[0] - 1
    total = before[last:, :] + onehot[last:, :]
    carry_ref[...] = total
    cnt_ref[...] = total

    meta = jnp.zeros_like(logits)
    for slot, val in ((META_E1, i1.astype(F32)), (META_E2, i2.astype(F32)), (META_W1, w1), (META_W2, w2),
                      (META_R1, r1), (META_R2, r2)):
        meta = jnp.where(lane == slot, val, meta)
    meta_ref[0] = meta


def moe_router(x, gain, sc, sh, wr_pad, br_pad, tm=512):
    b, t, d = x.shape
    row = lambda i, j: (i, j, 0)
    const = lambda i, j: (0, 0)
    tri = jnp.tri(tm, k=-1, dtype=BF16)
    return pl.pallas_call(
        _router_body,
        out_shape=(jax.ShapeDtypeStruct((b, t, d // 2), jnp.uint32),
                   jax.ShapeDtypeStruct((b, t, LANES), F32),
                   jax.ShapeDtypeStruct((1, LANES), F32)),
        grid=(b, t // tm),
        in_specs=[
            pl.BlockSpec((1, tm, d), row),
            pl.BlockSpec((1, d), const),
            pl.BlockSpec((1, 1, d), lambda i, j: (i, 0, 0)),
            pl.BlockSpec((1, 1, d), lambda i, j: (i, 0, 0)),
            pl.BlockSpec((d, LANES), const),
            pl.BlockSpec((1, LANES), const),
            pl.BlockSpec((tm, tm), const),
        ],
        out_specs=(pl.BlockSpec((1, tm, d // 2), row), pl.BlockSpec((1, tm, LANES), row),
                   pl.BlockSpec((1, LANES), const)),
        scratch_shapes=[pltpu.VMEM((1, LANES), F32)],
        compiler_params=_cparams(("arbitrary", "arbitrary")),
        name="moe_router",
    )(x, gain, sc, sh, wr_pad, br_pad, tri)


def _row_copy(src_ref, src_row, dst_ref, dst_row, sem):
    return pltpu.make_async_copy(src_ref.at[pl.ds(src_row, 1), :], dst_ref.at[pl.ds(dst_row, 1), :], sem)


def _dispatch_body(dest_ref, hp_ref, xs_in_ref, xs_ref, sem, *, tm):
    del xs_in_ref

    def issue(r, carry):
        _row_copy(hp_ref, r, xs_ref, dest_ref[0, 0, r], sem).start(priority=0)
        _row_copy(hp_ref, r, xs_ref, dest_ref[0, 0, tm + r], sem).start(priority=1)
        return carry

    lax.fori_loop(0, tm, issue, 0, unroll=8)

    def drain(r, carry):
        _row_copy(hp_ref, r, xs_ref, dest_ref[0, 0, r], sem).wait()
        _row_copy(hp_ref, r, xs_ref, dest_ref[0, 0, tm + r], sem).wait()
        return carry

    lax.fori_loop(0, tm, drain, 0, unroll=8)


def moe_dispatch(hp, dest, rows, tm):
    n, half = hp.shape
    body = functools.partial(_dispatch_body, tm=tm)
    return pl.pallas_call(
        body,
        out_shape=jax.ShapeDtypeStruct((rows, half), jnp.uint32),
        grid=(n // tm,),
        in_specs=[
            pl.BlockSpec((1, 1, 2 * tm), lambda i: (i, 0, 0), memory_space=pltpu.SMEM),
            pl.BlockSpec((tm, half), lambda i: (i, 0)),
            pl.BlockSpec(memory_space=pl.ANY),
        ],
        out_specs=pl.BlockSpec(memory_space=pl.ANY),
        scratch_shapes=[pltpu.SemaphoreType.DMA],
        input_output_aliases={2: 0},
        compiler_params=pltpu.CompilerParams(dimension_semantics=("arbitrary",), vmem_limit_bytes=VMEM_LIMIT,
                                             disable_bounds_checks=True),
        name="moe_dispatch",
    )(dest, hp, jnp.zeros((rows, half), jnp.uint32))


def _experts_body(te_ref, nu_ref, xs_ref, wg_ref, wu_ref, wd_ref, y_ref, h_ref, acc_ref):
    del te_ref
    i = pl.program_id(0)
    f = pl.program_id(1)

    @pl.when(i < nu_ref[0])
    def _():
        @pl.when(f == 0)
        def _():
            xp = xs_ref[...]
            lo = lax.bitcast_convert_type(xp << 16, F32).astype(BF16)
            hi = lax.bitcast_convert_type(xp & jnp.uint32(0xFFFF0000), F32).astype(BF16)
            h_ref[...] = jnp.concatenate([lo, hi], axis=-1)
            acc_ref[...] = jnp.zeros_like(acc_ref)

        h = h_ref[...]
        a = (_silu(_dot(h, wg_ref[0, 0].astype(BF16))) * _dot(h, wu_ref[0, 0].astype(BF16))).astype(BF16)
        acc_ref[...] += _dot(a, wd_ref[0, 0].astype(BF16))

        @pl.when(f == pl.num_programs(1) - 1)
        def _():
            y_ref[...] = acc_ref[...]

    @pl.when((i >= nu_ref[0]) & (f == pl.num_programs(1) - 1))
    def _():
        y_ref[...] = jnp.zeros_like(y_ref)


def moe_experts(xs, tile_expert, n_used, w_gate, w_up, w_down, layer, tr, tf=512):
    rows, half = xs.shape
    _, ne, d, ff = w_gate.shape
    nf = ff // tf
    tile = lambda i, nu: jnp.maximum(jnp.minimum(i, nu[0] - 1), 0)
    fblk = lambda i, f, nu: jnp.where(i < nu[0], f, nf - 1)
    return pl.pallas_call(
        _experts_body,
        out_shape=jax.ShapeDtypeStruct((rows, d), F32),
        grid_spec=pltpu.PrefetchScalarGridSpec(
            num_scalar_prefetch=2,
            grid=(rows // tr, nf),
            in_specs=[
                pl.BlockSpec((tr, half), lambda i, f, te, nu: (tile(i, nu), 0)),
                pl.BlockSpec((1, 1, d, tf), lambda i, f, te, nu: (layer, te[tile(i, nu)], 0, fblk(i, f, nu))),
                pl.BlockSpec((1, 1, d, tf), lambda i, f, te, nu: (layer, te[tile(i, nu)], 0, fblk(i, f, nu))),
                pl.BlockSpec((1, 1, tf, d), lambda i, f, te, nu: (layer, te[tile(i, nu)], fblk(i, f, nu), 0)),
            ],
            out_specs=pl.BlockSpec((tr, d), lambda i, f, te, nu: (i, 0)),
            scratch_shapes=[pltpu.VMEM((tr, d), BF16), pltpu.VMEM((tr, d), F32)],
        ),
        compiler_params=_cparams(("arbitrary", "arbitrary")),
        name="moe_experts",
    )(tile_expert, n_used, xs, w_gate, w_up, w_down)


def _combine_body(dest_ref, x_ref, g_ref, meta_ref, y_ref, o_ref, buf_ref, sem, *, tm):
    def issue(r, carry):
        _row_copy(y_ref, dest_ref[0, 0, r], buf_ref.at[0], r, sem).start(priority=0)
        _row_copy(y_ref, dest_ref[0, 0, tm + r], buf_ref.at[1], r, sem).start(priority=1)
        return carry

    lax.fori_loop(0, tm, issue, 0, unroll=8)

    def drain(r, carry):
        _row_copy(y_ref, dest_ref[0, 0, r], buf_ref.at[0], r, sem).wait()
        _row_copy(y_ref, dest_ref[0, 0, tm + r], buf_ref.at[1], r, sem).wait()
        return carry

    lax.fori_loop(0, tm, drain, 0, unroll=8)

    meta = meta_ref[...]
    lane = lax.broadcasted_iota(jnp.int32, meta.shape, 1)
    w1 = _lane_pick(meta, lane, META_W1)
    w2 = _lane_pick(meta, lane, META_W2)
    o_ref[...] = x_ref[...] + g_ref[0] * (w1 * buf_ref[0] + w2 * buf_ref[1])


def moe_combine(x2, g, meta2, y, dest, t, tm):
    n, d = x2.shape
    per_batch = t // tm
    body = functools.partial(_combine_body, tm=tm)
    return pl.pallas_call(
        body,
        out_shape=jax.ShapeDtypeStruct((n, d), F32),
        grid=(n // tm,),
        in_specs=[
            pl.BlockSpec((1, 1, 2 * tm), lambda i: (i, 0, 0), memory_space=pltpu.SMEM),
            pl.BlockSpec((tm, d), lambda i: (i, 0)),
            pl.BlockSpec((1, 1, d), lambda i: (i // per_batch, 0, 0)),
            pl.BlockSpec((tm, LANES), lambda i: (i, 0)),
            pl.BlockSpec(memory_space=pl.ANY),
        ],
        out_specs=pl.BlockSpec((tm, d), lambda i: (i, 0)),
        scratch_shapes=[pltpu.VMEM((2, tm, d), F32), pltpu.SemaphoreType.DMA],
        compiler_params=pltpu.CompilerParams(dimension_semantics=("arbitrary",), vmem_limit_bytes=VMEM_LIMIT,
                                             disable_bounds_checks=True),
        name="moe_combine",
    )(dest, x2, g, meta2, y)


def moe_plan(meta2, counts, tr, tm):
    n = meta2.shape[0]
    e1 = meta2[:, META_E1].astype(jnp.int32)
    e2 = meta2[:, META_E2].astype(jnp.int32)
    cnt = counts[0, :N_EXPERTS].astype(jnp.int32)
    padded = (cnt + tr - 1) // tr * tr
    ends = jnp.cumsum(padded)
    starts = ends - padded
    d1 = starts[e1] + meta2[:, META_R1].astype(jnp.int32)
    d2 = starts[e2] + meta2[:, META_R2].astype(jnp.int32)
    dest = jnp.concatenate([d1.reshape(n // tm, 1, tm), d2.reshape(n // tm, 1, tm)], axis=-1)
    n_tiles = 2 * n // tr + N_EXPERTS
    tile_start = jnp.arange(n_tiles, dtype=jnp.int32) * tr
    tile_expert = jnp.sum((tile_start[:, None] >= ends[None, :]).astype(jnp.int32), axis=1)
    tile_expert = jnp.minimum(tile_expert, N_EXPERTS - 1)
    n_used = (ends[-1:] // tr).astype(jnp.int32)
    return dest, tile_expert, n_used, n_tiles * tr


def _pad_cols(w, n):
    return jnp.pad(w, ((0, 0), (0, n - w.shape[1])))


def kernel(x, c, ada_w, ada_b, norm_gain, gla_w_in, gla_w_gate_up, gla_b_gate, gla_o_norm, gla_w_out,
           fox_w_in, fox_b_f, fox_q_norm, fox_k_norm, fox_w_out, ffn_w_gate, ffn_w_up, ffn_w_down,
           moe_w_router, moe_b_router, moe_w_gate, moe_w_up, moe_w_down):
    b, t, d = x.shape
    depth = ada_w.shape[0]
    gla_rank = gla_w_gate_up.shape[1]
    dk = gla_w_gate_up.shape[2]
    dv = gla_w_out.shape[1]
    fox_heads = fox_b_f.shape[1]

    c_pad = jnp.pad(c, ((0, 8 - b), (0, 0)))
    mods = ada_mods(c_pad, ada_w, ada_b)[:, :b]
    mods = mods.reshape(depth, b, 6, 1, d)

    for i in range(depth):
        j = i // 2
        sh1, sc1, g1, sh2, sc2, g2 = (mods[i, :, m] for m in range(6))
        gain1 = norm_gain[i, 0].reshape(1, d)
        gain2 = norm_gain[i, 1].reshape(1, d)
        if i % 2 == 0:
            n_main = 2 * dk + 2 * dv
            w_pad = _pad_cols(gla_w_in[j], n_main + LANES).astype(BF16)
            wg_pad = jnp.pad(gla_w_gate_up[j], ((0, LANES - gla_rank), (0, 0))).astype(BF16)
            qk, v, r, la = gla_in_proj(x, gain1, sc1, sh1, w_pad, wg_pad, gla_b_gate[j].reshape(1, dk),
                                       dk=dk, dv=dv)
            a = gla_mix(qk, v, r, la, gla_o_norm[j].reshape(1, -1), heads=GLA_HEADS)
            x = out_proj_residual(a, gla_w_out[j].astype(BF16), x, g1)
            x = ffn_dense(x, gain2, sc2, sh2, g2, ffn_w_gate[j].astype(BF16), ffn_w_up[j].astype(BF16),
                          ffn_w_down[j].astype(BF16))
        else:
            w_pad = _pad_cols(fox_w_in[j], 4 * d + LANES).astype(BF16)
            bf_pad = jnp.pad(fox_b_f[j], (0, LANES - fox_heads)).reshape(1, LANES)
            qn_t = jnp.tile(fox_q_norm[j], fox_heads).reshape(1, d)
            kn_t = jnp.tile(fox_k_norm[j], fox_heads).reshape(1, d)
            q, k, kf, v, sg = fox_in_proj(x, gain1, sc1, sh1, w_pad, bf_pad, qn_t, kn_t, fox_heads)
            a = fox_attention(q, k, kf, v, sg)
            x = out_proj_residual(a, fox_w_out[j].astype(BF16), x, g1)
            wr_pad = _pad_cols(moe_w_router[j], LANES)
            br_pad = jnp.pad(moe_b_router[j], (0, LANES - N_EXPERTS)).reshape(1, LANES)
            hp, meta, counts = moe_router(x, gain2, sc2, sh2, wr_pad, br_pad)
            meta2 = meta.reshape(b * t, LANES)
            dest, tile_expert, n_used, rows = moe_plan(meta2, counts, MOE_ROW_TILE, MOE_TOKEN_TILE)
            xs = moe_dispatch(hp.reshape(b * t, d // 2), dest, rows, MOE_TOKEN_TILE)
            y = moe_experts(xs, tile_expert, n_used, moe_w_gate, moe_w_up, moe_w_down, j, MOE_ROW_TILE)
            x = moe_combine(x.reshape(b * t, d), g2, meta2, y, dest, t, MOE_TOKEN_TILE).reshape(b, t, d)
    return x
```

```python
import functools

import jax
import jax.numpy as jnp
from jax import lax
from jax.experimental import pallas as pl
from jax.experimental.pallas import tpu as pltpu

F32 = jnp.float32
BF16 = jnp.bfloat16

EPS = 1e-6
GLA_HEADS = 4
GLA_GATE_TAU = 16.0
GLA_CHUNK = 64
FOX_HEAD_DIM = 64
LOG2E = 1.4426950408889634
N_EXPERTS = 8
MOE_ROW_TILE = 1024
MOE_TOKEN_TILE = 512
LANES = 128
VMEM_LIMIT = 56 * 1024 * 1024


def _cparams(sem):
    return pltpu.CompilerParams(dimension_semantics=sem, vmem_limit_bytes=VMEM_LIMIT)


def _dot(a, b):
    return jnp.dot(a, b, preferred_element_type=F32)


def _dot_nt(a, b):
    return lax.dot_general(a, b, (((1,), (1,)), ((), ())), preferred_element_type=F32)


def _dot_tn(a, b):
    return lax.dot_general(a, b, (((0,), (0,)), ((), ())), preferred_element_type=F32)


def _split3(a):
    a1 = a.astype(BF16)
    r1 = a - a1.astype(F32)
    a2 = r1.astype(BF16)
    a3 = (r1 - a2.astype(F32)).astype(BF16)
    return a1, a2, a3


def _exact_left_dot(m01, a):
    a1, a2, a3 = _split3(a)
    return _dot(m01, a1) + _dot(m01, a2) + _dot(m01, a3)


def _log_sigmoid(x):
    return jnp.minimum(x, 0.0) - jnp.log1p(jnp.exp(-jnp.abs(x)))


def _sigmoid(x):
    return 1.0 / (1.0 + jnp.exp(-x))


def _silu(x):
    return x * _sigmoid(x)


def _norm_mod(x, gain, scale, shift):
    y = x * lax.rsqrt(jnp.mean(x * x, axis=-1, keepdims=True) + EPS)
    return (y * gain) * (1.0 + scale) + shift


def _ada_body(c_ref, w_ref, b_ref, o_ref):
    cond = _silu(c_ref[...]).astype(BF16)
    o_ref[0] = _dot(cond, w_ref[0].astype(BF16)) + b_ref[0]


def ada_mods(c_pad, ada_w, ada_b, tn=1536):
    depth, d, n = ada_w.shape
    rows = c_pad.shape[0]
    return pl.pallas_call(
        _ada_body,
        out_shape=jax.ShapeDtypeStruct((depth, rows, n), F32),
        grid=(depth, n // tn),
        in_specs=[
            pl.BlockSpec((rows, d), lambda i, j: (0, 0)),
            pl.BlockSpec((1, d, tn), lambda i, j: (i, 0, j)),
            pl.BlockSpec((1, 1, tn), lambda i, j: (i, 0, j)),
        ],
        out_specs=pl.BlockSpec((1, rows, tn), lambda i, j: (i, 0, j)),
        compiler_params=_cparams(("parallel", "parallel")),
        name="ada_mods",
    )(c_pad, ada_w, ada_b.reshape(depth, 1, n))


def _gla_in_body(x_ref, gain_ref, sc_ref, sh_ref, w_ref, wg_ref, bg_ref,
                 qk_ref, v_ref, r_ref, la_ref, *, dk, dv):
    h = _norm_mod(x_ref[0], gain_ref[...], sc_ref[0], sh_ref[0]).astype(BF16)
    qk_ref[0] = _dot(h, w_ref[:, 0:2 * dk])
    v_ref[0] = _dot(h, w_ref[:, 2 * dk:2 * dk + dv]).astype(BF16)
    r_ref[0] = _dot(h, w_ref[:, 2 * dk + dv:2 * dk + 2 * dv]).astype(BF16)
    g_low = _dot(h, w_ref[:, 2 * dk + 2 * dv:]).astype(BF16)
    gate = _dot(g_low, wg_ref[...]) + bg_ref[...]
    la_ref[0] = _log_sigmoid(gate) * (1.0 / GLA_GATE_TAU)


def gla_in_proj(x, gain, sc, sh, w_pad, wg_pad, b_gate, *, dk, dv, tm=512):
    b, t, d = x.shape
    n_pad = w_pad.shape[1]
    body = functools.partial(_gla_in_body, dk=dk, dv=dv)
    return pl.pallas_call(
        body,
        out_shape=(
            jax.ShapeDtypeStruct((b, t, 2 * dk), F32),
            jax.ShapeDtypeStruct((b, t, dv), BF16),
            jax.ShapeDtypeStruct((b, t, dv), BF16),
            jax.ShapeDtypeStruct((b, t, dk), F32),
        ),
        grid=(b, t // tm),
        in_specs=[
            pl.BlockSpec((1, tm, d), lambda i, j: (i, j, 0)),
            pl.BlockSpec((1, d), lambda i, j: (0, 0)),
            pl.BlockSpec((1, 1, d), lambda i, j: (i, 0, 0)),
            pl.BlockSpec((1, 1, d), lambda i, j: (i, 0, 0)),
            pl.BlockSpec((d, n_pad), lambda i, j: (0, 0)),
            pl.BlockSpec((LANES, dk), lambda i, j: (0, 0)),
            pl.BlockSpec((1, dk), lambda i, j: (0, 0)),
        ],
        out_specs=(
            pl.BlockSpec((1, tm, 2 * dk), lambda i, j: (i, j, 0)),
            pl.BlockSpec((1, tm, dv), lambda i, j: (i, j, 0)),
            pl.BlockSpec((1, tm, dv), lambda i, j: (i, j, 0)),
            pl.BlockSpec((1, tm, dk), lambda i, j: (i, j, 0)),
        ),
        compiler_params=_cparams(("parallel", "parallel")),
        name="gla_in_proj",
    )(x, gain, sc, sh, w_pad, wg_pad, b_gate)


def _gla_body(qk_ref, v_ref, la_ref, r_ref, on_ref, tri_ref, o_ref, st_ref, *, heads, hk, hv, n_chunks):
    @pl.when(pl.program_id(1) == 0)
    def _():
        st_ref[...] = jnp.zeros_like(st_ref)

    c = GLA_CHUNK
    dk = heads * hk
    tri = tri_ref[...]
    in_chunk_causal = tri > 0
    bcum = _exact_left_dot(tri, la_ref[0])
    b_last = [bcum[ci * c + c - 1:ci * c + c, :] for ci in range(n_chunks)]
    b_end = jnp.concatenate([jnp.broadcast_to(bl, (c, dk)) for bl in b_last], axis=0)
    q = qk_ref[0, :, 0:dk] * (hk ** -0.5)
    k = qk_ref[0, :, dk:2 * dk]
    q_dec = (q * jnp.exp(bcum)).astype(BF16)
    k_inv = (k * jnp.exp(-bcum)).astype(BF16)
    k_end = (k * jnp.exp(b_end - bcum)).astype(BF16)
    for hh in range(heads):
        ksl = slice(hh * hk, (hh + 1) * hk)
        vsl = slice(hh * hv, (hh + 1) * hv)
        v = v_ref[0, :, vsl]
        attn = jnp.where(in_chunk_causal, _dot_nt(q_dec[:, ksl], k_inv[:, ksl]), 0.0).astype(BF16)
        o_intra = _dot(attn, v)
        st = st_ref[hh]
        o_inter = []
        for ci in range(n_chunks):
            sl = slice(ci * c, (ci + 1) * c)
            o_inter.append(_dot_nt(q_dec[sl, ksl], st.astype(BF16)))
            st = jnp.exp(b_last[ci][:, ksl]) * st + _dot_tn(v[sl, :], k_end[sl, ksl])
        st_ref[hh] = st
        o = o_intra + jnp.concatenate(o_inter, axis=0)
        o_n = o * lax.rsqrt(jnp.mean(o * o, axis=-1, keepdims=True) + EPS) * on_ref[...]
        o_ref[0, :, vsl] = (o_n * _silu(r_ref[0, :, vsl].astype(F32))).astype(BF16)


def gla_mix(qk, v, r, la, o_norm, *, heads, tt=256):
    b, t, dk2 = qk.shape
    dk = dk2 // 2
    dv = v.shape[-1]
    hk, hv = dk // heads, dv // heads
    tri = jnp.kron(jnp.eye(tt // GLA_CHUNK, dtype=F32), jnp.tri(GLA_CHUNK, dtype=F32)).astype(BF16)
    body = functools.partial(_gla_body, heads=heads, hk=hk, hv=hv, n_chunks=tt // GLA_CHUNK)
    row = lambda i, j: (i, j, 0)
    return pl.pallas_call(
        body,
        out_shape=jax.ShapeDtypeStruct((b, t, dv), BF16),
        grid=(b, t // tt),
        in_specs=[
            pl.BlockSpec((1, tt, dk2), row),
            pl.BlockSpec((1, tt, dv), row),
            pl.BlockSpec((1, tt, dk), row),
            pl.BlockSpec((1, tt, dv), row),
            pl.BlockSpec((1, hv), lambda i, j: (0, 0)),
            pl.BlockSpec((tt, tt), lambda i, j: (0, 0)),
        ],
        out_specs=pl.BlockSpec((1, tt, dv), row),
        scratch_shapes=[pltpu.VMEM((heads, hv, hk), F32)],
        compiler_params=_cparams(("parallel", "arbitrary")),
        name="gla_mix",
    )(qk, v, la, r, o_norm, tri)


def _out_proj_body(a_ref, w_ref, x_ref, g_ref, o_ref):
    o_ref[0] = x_ref[0] + g_ref[0] * _dot(a_ref[0], w_ref[...])


def out_proj_residual(a, w, x, g, tm=1024):
    b, t, d = x.shape
    k = a.shape[-1]
    return pl.pallas_call(
        _out_proj_body,
        out_shape=jax.ShapeDtypeStruct((b, t, d), F32),
        grid=(b, t // tm),
        in_specs=[
            pl.BlockSpec((1, tm, k), lambda i, j: (i, j, 0)),
            pl.BlockSpec((k, d), lambda i, j: (0, 0)),
            pl.BlockSpec((1, tm, d), lambda i, j: (i, j, 0)),
            pl.BlockSpec((1, 1, d), lambda i, j: (i, 0, 0)),
        ],
        out_specs=pl.BlockSpec((1, tm, d), lambda i, j: (i, j, 0)),
        compiler_params=_cparams(("parallel", "parallel")),
        name="out_proj_residual",
    )(a, w, x, g)


def _ffn_body(x_ref, gain_ref, sc_ref, sh_ref, g_ref, wg_ref, wu_ref, wd_ref, o_ref, h_ref, acc_ref):
    f = pl.program_id(2)

    @pl.when(f == 0)
    def _():
        h_ref[...] = _norm_mod(x_ref[0], gain_ref[...], sc_ref[0], sh_ref[0]).astype(BF16)
        acc_ref[...] = jnp.zeros_like(acc_ref)

    h = h_ref[...]
    a = (_silu(_dot(h, wg_ref[...])) * _dot(h, wu_ref[...])).astype(BF16)
    acc_ref[...] += _dot(a, wd_ref[...])

    @pl.when(f == pl.num_programs(2) - 1)
    def _():
        o_ref[0] = x_ref[0] + g_ref[0] * acc_ref[...]


def ffn_dense(x, gain, sc, sh, g, w_gate, w_up, w_down, tm=1024, tf=512):
    b, t, d = x.shape
    ff = w_gate.shape[1]
    return pl.pallas_call(
        _ffn_body,
        out_shape=jax.ShapeDtypeStruct((b, t, d), F32),
        grid=(b, t // tm, ff // tf),
        in_specs=[
            pl.BlockSpec((1, tm, d), lambda i, j, f: (i, j, 0)),
            pl.BlockSpec((1, d), lambda i, j, f: (0, 0)),
            pl.BlockSpec((1, 1, d), lambda i, j, f: (i, 0, 0)),
            pl.BlockSpec((1, 1, d), lambda i, j, f: (i, 0, 0)),
            pl.BlockSpec((1, 1, d), lambda i, j, f: (i, 0, 0)),
            pl.BlockSpec((d, tf), lambda i, j, f: (0, f)),
            pl.BlockSpec((d, tf), lambda i, j, f: (0, f)),
            pl.BlockSpec((tf, d), lambda i, j, f: (f, 0)),
        ],
        out_specs=pl.BlockSpec((1, tm, d), lambda i, j, f: (i, j, 0)),
        scratch_shapes=[pltpu.VMEM((tm, d), BF16), pltpu.VMEM((tm, d), F32)],
        compiler_params=_cparams(("parallel", "parallel", "arbitrary")),
        name="ffn_dense",
    )(x, gain, sc, sh, g, w_gate, w_up, w_down)


def _group_rms(x, gmean):
    sq = x * x
    hi = sq.astype(BF16)
    lo = (sq - hi.astype(F32)).astype(BF16)
    cols = gmean.shape[0]
    parts = []
    for s in range(0, x.shape[-1], cols):
        parts.append(_dot(hi[:, s:s + cols], gmean) + _dot(lo[:, s:s + cols], gmean))
    ms = jnp.concatenate(parts, axis=-1)
    return x * lax.rsqrt(ms + EPS)


def _fox_in_body(x_ref, gain_ref, sc_ref, sh_ref, w_ref, bf_ref, qn_ref, kn_ref, gm_ref, tri_ref, pf_ref,
                 q_ref, k_ref, kf_ref, v_ref, sg_ref, carry_ref, *, d, heads):
    @pl.when(pl.program_id(1) == 0)
    def _():
        carry_ref[...] = jnp.zeros_like(carry_ref)

    h = _norm_mod(x_ref[0], gain_ref[...], sc_ref[0], sh_ref[0]).astype(BF16)
    gm = gm_ref[...]
    q = _group_rms(_dot(h, w_ref[:, 0:d]), gm) * qn_ref[...]
    q_ref[0] = (q * (FOX_HEAD_DIM ** -0.5 * LOG2E)).astype(BF16)
    k = _group_rms(_dot(h, w_ref[:, d:2 * d]), gm) * kn_ref[...]
    k_ref[0] = k.astype(BF16)
    v_ref[0, 0] = _dot(h, w_ref[:, 2 * d:3 * d]).T.astype(BF16)
    sg_ref[0] = _sigmoid(_dot(h, w_ref[:, 3 * d:4 * d])).astype(BF16)
    log_f = _log_sigmoid(_dot(h, w_ref[:, 4 * d:]) + bf_ref[...])
    fcum = _exact_left_dot(tri_ref[...], log_f) + carry_ref[...]
    carry_ref[...] = fcum[fcum.shape[0] - 1:, :]
    lane = lax.broadcasted_iota(jnp.int32, fcum.shape, 1)
    p1, p2, p3 = _split3(jnp.where(lane < heads, fcum * (-LOG2E), 0.0))
    packed = (p1.astype(F32) + pltpu.roll(p2.astype(F32), heads, 1)
              + pltpu.roll(p3.astype(F32), 2 * heads, 1)).astype(BF16)
    kf_ref[0] = _dot(packed, pf_ref[...]).astype(BF16)


def _fox_bias_placement(heads, d):
    src, dst = [], []
    for i in range(3):
        for hd in range(heads):
            src.append(i * heads + hd)
            dst.append((hd // 2) * LANES + 3 * (hd % 2) + i)
    return jnp.zeros((LANES, d), F32).at[jnp.array(src), jnp.array(dst)].set(1.0).astype(BF16)


def fox_in_proj(x, gain, sc, sh, w_pad, bf_pad, qn_t, kn_t, heads, tm=512):
    b, t, d = x.shape
    n_pad = w_pad.shape[1]
    gcols = 2 * LANES
    gm = (jnp.kron(jnp.eye(gcols // FOX_HEAD_DIM, dtype=F32),
                   jnp.ones((FOX_HEAD_DIM, FOX_HEAD_DIM), F32)) / FOX_HEAD_DIM).astype(BF16)
    tri = jnp.tri(tm, dtype=BF16)
    pf = _fox_bias_placement(heads, d)
    body = functools.partial(_fox_in_body, d=d, heads=heads)
    bf = lambda s: jax.ShapeDtypeStruct(s, BF16)
    row = lambda i, j: (i, j, 0)
    const = lambda i, j: (0, 0)
    return pl.pallas_call(
        body,
        out_shape=(bf((b, t, d)), bf((b, t, d)), bf((b, t, d)), bf((b, t // tm, d, tm)), bf((b, t, d))),
        grid=(b, t // tm),
        in_specs=[
            pl.BlockSpec((1, tm, d), row),
            pl.BlockSpec((1, d), const),
            pl.BlockSpec((1, 1, d), lambda i, j: (i, 0, 0)),
            pl.BlockSpec((1, 1, d), lambda i, j: (i, 0, 0)),
            pl.BlockSpec((d, n_pad), const),
            pl.BlockSpec((1, LANES), const),
            pl.BlockSpec((1, d), const),
            pl.BlockSpec((1, d), const),
            pl.BlockSpec((gcols, gcols), const),
            pl.BlockSpec((tm, tm), const),
            pl.BlockSpec((LANES, d), const),
        ],
        out_specs=(pl.BlockSpec((1, tm, d), row), pl.BlockSpec((1, tm, d), row), pl.BlockSpec((1, tm, d), row),
                   pl.BlockSpec((1, 1, d, tm), lambda i, j: (i, j, 0, 0)), pl.BlockSpec((1, tm, d), row)),
        scratch_shapes=[pltpu.VMEM((1, LANES), F32)],
        compiler_params=_cparams(("parallel", "arbitrary")),
        name="fox_in_proj",
    )(x, gain, sc, sh, w_pad, bf_pad, qn_t, kn_t, gm, tri, pf)


def _fox_attn_body(q_ref, k_ref, kf_ref, vt_ref, sg_ref, o_ref, s0_ref, s1_ref, mx0_ref, mx1_ref,
                   m_ref, l_ref, acc_ref, *, tq, tk):
    i = pl.program_id(2)
    hd = FOX_HEAD_DIM
    lane = lax.broadcasted_iota(jnp.int32, (1, LANES), 1)
    q2 = q_ref[0]
    qs = []
    for hh in range(2):
        qh = jnp.where((lane >= hh * hd) & (lane < (hh + 1) * hd), q2, jnp.zeros_like(q2))
        ones = jnp.where((lane >= 3 * hh) & (lane < 3 * hh + 3), 1.0, 0.0).astype(BF16)
        qs.append(jnp.concatenate([qh, jnp.broadcast_to(ones, qh.shape)], axis=-1))

    m_ref[...] = jnp.full_like(m_ref, -jnp.inf)
    l_ref[...] = jnp.zeros_like(l_ref)
    acc_ref[...] = jnp.zeros_like(acc_ref)

    def scores(j, s_ref, mx_ref, diagonal):
        ks = pl.multiple_of(j * tk, tk)
        kcat = jnp.concatenate([k_ref[0, pl.ds(ks, tk), :], kf_ref[0, pl.ds(ks, tk), :]], axis=-1)
        for hh in range(2):
            st = _dot_nt(kcat, qs[hh])
            if diagonal:
                key = lax.broadcasted_iota(jnp.int32, (tk, tq), 0)
                qry = lax.broadcasted_iota(jnp.int32, (tk, tq), 1)
                st = jnp.where(key <= qry, st, -jnp.inf)
            s_ref[hh] = st
            mx_ref[hh] = jnp.max(st, axis=0, keepdims=True)

    def consume(j, s_ref, mx_ref):
        vt = vt_ref[0, j]
        for hh in range(2):
            m_prev = m_ref[hh]
            m_new = jnp.maximum(m_prev, mx_ref[hh])
            alpha = jnp.exp2(m_prev - m_new)
            p = jnp.exp2(s_ref[hh] - m_new)
            l_ref[hh] = alpha * l_ref[hh] + jnp.sum(p, axis=0, keepdims=True)
            acc_ref[hh] = alpha * acc_ref[hh] + _dot(vt, p.astype(BF16))
            m_ref[hh] = m_new

    scores(i, s0_ref, mx0_ref, True)

    def pair(u, carry):
        t = 2 * u
        scores(t, s1_ref, mx1_ref, False)
        consume(jnp.where(u == 0, i, t - 1), s0_ref, mx0_ref)
        scores(t + 1, s0_ref, mx0_ref, False)
        consume(t, s1_ref, mx1_ref)
        return carry

    lax.fori_loop(0, i // 2, pair, 0)

    @pl.when(i % 2 == 1)
    def _():
        scores(i - 1, s1_ref, mx1_ref, False)
        consume(jnp.where(i == 1, i, i - 2), s0_ref, mx0_ref)
        consume(i - 1, s1_ref, mx1_ref)

    @pl.when(i % 2 == 0)
    def _():
        consume(jnp.maximum(i - 1, 0), s0_ref, mx0_ref)

    chan = lax.broadcasted_iota(jnp.int32, (2 * hd, 1), 0)
    o_t = jnp.where(chan < hd, acc_ref[0] * (1.0 / l_ref[0]), acc_ref[1] * (1.0 / l_ref[1]))
    o_ref[0] = (o_t.T * sg_ref[0].astype(F32)).astype(BF16)


def fox_attention(q, k, kf, vt, sg, tq=512):
    b, t, d = q.shape
    pw = 2 * FOX_HEAD_DIM
    pairs = d // pw
    nk, tk = vt.shape[1], vt.shape[3]
    assert tk == tq
    body = functools.partial(_fox_attn_body, tq=tq, tk=tk)
    qmap = lambda bi, p, i: (bi, i, p)
    kmap = lambda bi, p, i: (bi, 0, p)
    return pl.pallas_call(
        body,
        out_shape=jax.ShapeDtypeStruct((b, t, d), BF16),
        grid=(b, pairs, t // tq),
        in_specs=[
            pl.BlockSpec((1, tq, pw), qmap),
            pl.BlockSpec((1, t, pw), kmap),
            pl.BlockSpec((1, t, pw), kmap),
            pl.BlockSpec((1, nk, pw, tk), lambda bi, p, i: (bi, 0, p, 0)),
            pl.BlockSpec((1, tq, pw), qmap),
        ],
        out_specs=pl.BlockSpec((1, tq, pw), qmap),
        scratch_shapes=[
            pltpu.VMEM((2, tk, tq), F32),
            pltpu.VMEM((2, tk, tq), F32),
            pltpu.VMEM((2, 1, tq), F32),
            pltpu.VMEM((2, 1, tq), F32),
            pltpu.VMEM((2, 1, tq), F32),
            pltpu.VMEM((2, 1, tq), F32),
            pltpu.VMEM((2, pw, tq), F32),
        ],
        compiler_params=_cparams(("parallel", "parallel", "arbitrary")),
        name="fox_attention",
    )(q, k, kf, vt, sg)


META_E1, META_E2, META_W1, META_W2, META_R1, META_R2 = range(6)


def _lane_pick(a, lane, idx):
    return jnp.sum(jnp.where(lane == idx, a, 0.0), axis=-1, keepdims=True)


def _router_body(x_ref, gain_ref, sc_ref, sh_ref, wr_ref, br_ref, tri_ref, hp_ref, meta_ref, cnt_ref, carry_ref):
    @pl.when((pl.program_id(0) == 0) & (pl.program_id(1) == 0))
    def _():
        carry_ref[...] = jnp.zeros_like(carry_ref)

    h = _norm_mod(x_ref[0], gain_ref[...], sc_ref[0], sh_ref[0])
    half = h.shape[-1] // 2
    hb = h.astype(BF16).astype(F32)
    lo = lax.bitcast_convert_type(hb[:, :half], jnp.uint32) >> 16
    hi = lax.bitcast_convert_type(hb[:, half:], jnp.uint32) & jnp.uint32(0xFFFF0000)
    hp_ref[0] = hi | lo

    logits = jnp.dot(h, wr_ref[...], preferred_element_type=F32,
                     precision=lax.Precision.HIGHEST) + br_ref[...]
    lane = lax.broadcasted_iota(jnp.int32, logits.shape, 1)
    logits = jnp.where(lane < N_EXPERTS, logits, -jnp.inf)
    m1 = jnp.max(logits, axis=-1, keepdims=True)
    i1 = jnp.min(jnp.where(logits == m1, lane, LANES), axis=-1, keepdims=True)
    rest = jnp.where(lane == i1, -jnp.inf, logits)
    m2 = jnp.max(rest, axis=-1, keepdims=True)
    i2 = jnp.min(jnp.where(rest == m2, lane, LANES), axis=-1, keepdims=True)
    e2 = jnp.exp(m2 - m1)
    w1 = 1.0 / (1.0 + e2)
    w2 = e2 / (1.0 + e2)

    onehot = jnp.where(lane == i1, 1.0, 0.0) + jnp.where(lane == i2, 1.0, 0.0)
    before = _dot(tri_ref[...], onehot.astype(BF16)) + carry_ref[...]
    r1 = _lane_pick(before, lane, i1)
    r2 = _lane_pick(before, lane, i2)
    last = onehot.shape[0] - 1
    total = before[last:, :] + onehot[last:, :]
    carry_ref[...] = total
    cnt_ref[...] = total

    meta = jnp.zeros_like(logits)
    for slot, val in ((META_E1, i1.astype(F32)), (META_E2, i2.astype(F32)), (META_W1, w1), (META_W2, w2),
                      (META_R1, r1), (META_R2, r2)):
        meta = jnp.where(lane == slot, val, meta)
    meta_ref[0] = meta


def moe_router(x, gain, sc, sh, wr_pad, br_pad, tm=512):
    b, t, d = x.shape
    row = lambda i, j: (i, j, 0)
    const = lambda i, j: (0, 0)
    tri = jnp.tri(tm, k=-1, dtype=BF16)
    return pl.pallas_call(
        _router_body,
        out_shape=(jax.ShapeDtypeStruct((b, t, d // 2), jnp.uint32),
                   jax.ShapeDtypeStruct((b, t, LANES), F32),
                   jax.ShapeDtypeStruct((1, LANES), F32)),
        grid=(b, t // tm),
        in_specs=[
            pl.BlockSpec((1, tm, d), row),
            pl.BlockSpec((1, d), const),
            pl.BlockSpec((1, 1, d), lambda i, j: (i, 0, 0)),
            pl.BlockSpec((1, 1, d), lambda i, j: (i, 0, 0)),
            pl.BlockSpec((d, LANES), const),
            pl.BlockSpec((1, LANES), const),
            pl.BlockSpec((tm, tm), const),
        ],
        out_specs=(pl.BlockSpec((1, tm, d // 2), row), pl.BlockSpec((1, tm, LANES), row),
                   pl.BlockSpec((1, LANES), const)),
        scratch_shapes=[pltpu.VMEM((1, LANES), F32)],
        compiler_params=_cparams(("arbitrary", "arbitrary")),
        name="moe_router",
    )(x, gain, sc, sh, wr_pad, br_pad, tri)


def _row_copy(src_ref, src_row, dst_ref, dst_row, sem):
    return pltpu.make_async_copy(src_ref.at[pl.ds(src_row, 1), :], dst_ref.at[pl.ds(dst_row, 1), :], sem)


def _dispatch_body(dest_ref, hp_ref, xs_in_ref, xs_ref, sem, *, tm):
    del xs_in_ref

    for r in range(tm):
        _row_copy(hp_ref, r, xs_ref, dest_ref[0, 0, r], sem).start(priority=0)
        _row_copy(hp_ref, r, xs_ref, dest_ref[0, 0, tm + r], sem).start(priority=1)

    def drain(r, carry):
        _row_copy(hp_ref, r, xs_ref, dest_ref[0, 0, r], sem).wait()
        _row_copy(hp_ref, r, xs_ref, dest_ref[0, 0, tm + r], sem).wait()
        return carry

    lax.fori_loop(0, tm, drain, 0, unroll=8)


def moe_dispatch(hp, dest, rows, tm):
    n, half = hp.shape
    body = functools.partial(_dispatch_body, tm=tm)
    return pl.pallas_call(
        body,
        out_shape=jax.ShapeDtypeStruct((rows, half), jnp.uint32),
        grid=(n // tm,),
        in_specs=[
            pl.BlockSpec((1, 1, 2 * tm), lambda i: (i, 0, 0), memory_space=pltpu.SMEM),
            pl.BlockSpec((tm, half), lambda i: (i, 0)),
            pl.BlockSpec(memory_space=pl.ANY),
        ],
        out_specs=pl.BlockSpec(memory_space=pl.ANY),
        scratch_shapes=[pltpu.SemaphoreType.DMA],
        input_output_aliases={2: 0},
        compiler_params=pltpu.CompilerParams(dimension_semantics=("arbitrary",), vmem_limit_bytes=VMEM_LIMIT,
                                             disable_bounds_checks=True),
        name="moe_dispatch",
    )(dest, hp, jnp.zeros((rows, half), jnp.uint32))


def _experts_body(te_ref, nu_ref, xs_ref, wg_ref, wu_ref, wd_ref, y_ref, h_ref, acc_ref):
    del te_ref
    i = pl.program_id(0)
    f = pl.program_id(1)

    @pl.when(i < nu_ref[0])
    def _():
        @pl.when(f == 0)
        def _():
            xp = xs_ref[...]
            lo = lax.bitcast_convert_type(xp << 16, F32).astype(BF16)
            hi = lax.bitcast_convert_type(xp & jnp.uint32(0xFFFF0000), F32).astype(BF16)
            h_ref[...] = jnp.concatenate([lo, hi], axis=-1)
            acc_ref[...] = jnp.zeros_like(acc_ref)

        h = h_ref[...]
        a = (_silu(_dot(h, wg_ref[0, 0].astype(BF16))) * _dot(h, wu_ref[0, 0].astype(BF16))).astype(BF16)
        acc_ref[...] += _dot(a, wd_ref[0, 0].astype(BF16))

        @pl.when(f == pl.num_programs(1) - 1)
        def _():
            y_ref[...] = acc_ref[...]

    @pl.when((i >= nu_ref[0]) & (f == pl.num_programs(1) - 1))
    def _():
        y_ref[...] = jnp.zeros_like(y_ref)


def moe_experts(xs, tile_expert, n_used, w_gate, w_up, w_down, layer, tr, tf=512):
    rows, half = xs.shape
    _, ne, d, ff = w_gate.shape
    nf = ff // tf
    tile = lambda i, nu: jnp.maximum(jnp.minimum(i, nu[0] - 1), 0)
    fblk = lambda i, f, nu: jnp.where(i < nu[0], f, nf - 1)
    return pl.pallas_call(
        _experts_body,
        out_shape=jax.ShapeDtypeStruct((rows, d), F32),
        grid_spec=pltpu.PrefetchScalarGridSpec(
            num_scalar_prefetch=2,
            grid=(rows // tr, nf),
            in_specs=[
                pl.BlockSpec((tr, half), lambda i, f, te, nu: (tile(i, nu), 0)),
                pl.BlockSpec((1, 1, d, tf), lambda i, f, te, nu: (layer, te[tile(i, nu)], 0, fblk(i, f, nu))),
                pl.BlockSpec((1, 1, d, tf), lambda i, f, te, nu: (layer, te[tile(i, nu)], 0, fblk(i, f, nu))),
                pl.BlockSpec((1, 1, tf, d), lambda i, f, te, nu: (layer, te[tile(i, nu)], fblk(i, f, nu), 0)),
            ],
            out_specs=pl.BlockSpec((tr, d), lambda i, f, te, nu: (i, 0)),
            scratch_shapes=[pltpu.VMEM((tr, d), BF16), pltpu.VMEM((tr, d), F32)],
        ),
        compiler_params=_cparams(("arbitrary", "arbitrary")),
        name="moe_experts",
    )(tile_expert, n_used, xs, w_gate, w_up, w_down)


def _combine_body(dest_ref, x_ref, g_ref, meta_ref, y_ref, o_ref, buf_ref, sem, *, tm):
    for r in range(tm):
        _row_copy(y_ref, dest_ref[0, 0, r], buf_ref.at[0], r, sem).start(priority=0)
        _row_copy(y_ref, dest_ref[0, 0, tm + r], buf_ref.at[1], r, sem).start(priority=1)

    def drain(r, carry):
        _row_copy(y_ref, dest_ref[0, 0, r], buf_ref.at[0], r, sem).wait()
        _row_copy(y_ref, dest_ref[0, 0, tm + r], buf_ref.at[1], r, sem).wait()
        return carry

    lax.fori_loop(0, tm, drain, 0, unroll=8)

    meta = meta_ref[...]
    lane = lax.broadcasted_iota(jnp.int32, meta.shape, 1)
    w1 = _lane_pick(meta, lane, META_W1)
    w2 = _lane_pick(meta, lane, META_W2)
    o_ref[...] = x_ref[...] + g_ref[0] * (w1 * buf_ref[0] + w2 * buf_ref[1])


def moe_combine(x2, g, meta2, y, dest, t, tm):
    n, d = x2.shape
    per_batch = t // tm
    body = functools.partial(_combine_body, tm=tm)
    return pl.pallas_call(
        body,
        out_shape=jax.ShapeDtypeStruct((n, d), F32),
        grid=(n // tm,),
        in_specs=[
            pl.BlockSpec((1, 1, 2 * tm), lambda i: (i, 0, 0), memory_space=pltpu.SMEM),
            pl.BlockSpec((tm, d), lambda i: (i, 0)),
            pl.BlockSpec((1, 1, d), lambda i: (i // per_batch, 0, 0)),
            pl.BlockSpec((tm, LANES), lambda i: (i, 0)),
            pl.BlockSpec(memory_space=pl.ANY),
        ],
        out_specs=pl.BlockSpec((tm, d), lambda i: (i, 0)),
        scratch_shapes=[pltpu.VMEM((2, tm, d), F32), pltpu.SemaphoreType.DMA],
        compiler_params=pltpu.CompilerParams(dimension_semantics=("arbitrary",), vmem_limit_bytes=VMEM_LIMIT,
                                             disable_bounds_checks=True),
        name="moe_combine",
    )(dest, x2, g, meta2, y)


def moe_plan(meta2, counts, tr, tm):
    n = meta2.shape[0]
    e1 = meta2[:, META_E1].astype(jnp.int32)
    e2 = meta2[:, META_E2].astype(jnp.int32)
    cnt = counts[0, :N_EXPERTS].astype(jnp.int32)
    padded = (cnt + tr - 1) // tr * tr
    ends = jnp.cumsum(padded)
    starts = ends - padded
    d1 = starts[e1] + meta2[:, META_R1].astype(jnp.int32)
    d2 = starts[e2] + meta2[:, META_R2].astype(jnp.int32)
    dest = jnp.concatenate([d1.reshape(n // tm, 1, tm), d2.reshape(n // tm, 1, tm)], axis=-1)
    n_tiles = 2 * n // tr + N_EXPERTS
    tile_start = jnp.arange(n_tiles, dtype=jnp.int32) * tr
    tile_expert = jnp.sum((tile_start[:, None] >= ends[None, :]).astype(jnp.int32), axis=1)
    tile_expert = jnp.minimum(tile_expert, N_EXPERTS - 1)
    n_used = (ends[-1:] // tr).astype(jnp.int32)
    return dest, tile_expert, n_used, n_tiles * tr


def _pad_cols(w, n):
    return jnp.pad(w, ((0, 0), (0, n - w.shape[1])))


def kernel(x, c, ada_w, ada_b, norm_gain, gla_w_in, gla_w_gate_up, gla_b_gate, gla_o_norm, gla_w_out,
           fox_w_in, fox_b_f, fox_q_norm, fox_k_norm, fox_w_out, ffn_w_gate, ffn_w_up, ffn_w_down,
           moe_w_router, moe_b_router, moe_w_gate, moe_w_up, moe_w_down):
    b, t, d = x.shape
    depth = ada_w.shape[0]
    gla_rank = gla_w_gate_up.shape[1]
    dk = gla_w_gate_up.shape[2]
    dv = gla_w_out.shape[1]
    fox_heads = fox_b_f.shape[1]

    c_pad = jnp.pad(c, ((0, 8 - b), (0, 0)))
    mods = ada_mods(c_pad, ada_w, ada_b)[:, :b]
    mods = mods.reshape(depth, b, 6, 1, d)

    for i in range(depth):
        j = i // 2
        sh1, sc1, g1, sh2, sc2, g2 = (mods[i, :, m] for m in range(6))
        gain1 = norm_gain[i, 0].reshape(1, d)
        gain2 = norm_gain[i, 1].reshape(1, d)
        if i % 2 == 0:
            n_main = 2 * dk + 2 * dv
            w_pad = _pad_cols(gla_w_in[j], n_main + LANES).astype(BF16)
            wg_pad = jnp.pad(gla_w_gate_up[j], ((0, LANES - gla_rank), (0, 0))).astype(BF16)
            qk, v, r, la = gla_in_proj(x, gain1, sc1, sh1, w_pad, wg_pad, gla_b_gate[j].reshape(1, dk),
                                       dk=dk, dv=dv)
            a = gla_mix(qk, v, r, la, gla_o_norm[j].reshape(1, -1), heads=GLA_HEADS)
            x = out_proj_residual(a, gla_w_out[j].astype(BF16), x, g1)
            x = ffn_dense(x, gain2, sc2, sh2, g2, ffn_w_gate[j].astype(BF16), ffn_w_up[j].astype(BF16),
                          ffn_w_down[j].astype(BF16))
        else:
            w_pad = _pad_cols(fox_w_in[j], 4 * d + LANES).astype(BF16)
            bf_pad = jnp.pad(fox_b_f[j], (0, LANES - fox_heads)).reshape(1, LANES)
            qn_t = jnp.tile(fox_q_norm[j], fox_heads).reshape(1, d)
            kn_t = jnp.tile(fox_k_norm[j], fox_heads).reshape(1, d)
            q, k, kf, v, sg = fox_in_proj(x, gain1, sc1, sh1, w_pad, bf_pad, qn_t, kn_t, fox_heads)
            a = fox_attention(q, k, kf, v, sg)
            x = out_proj_residual(a, fox_w_out[j].astype(BF16), x, g1)
            wr_pad = _pad_cols(moe_w_router[j], LANES)
            br_pad = jnp.pad(moe_b_router[j], (0, LANES - N_EXPERTS)).reshape(1, LANES)
            hp, meta, counts = moe_router(x, gain2, sc2, sh2, wr_pad, br_pad)
            meta2 = meta.reshape(b * t, LANES)
            dest, tile_expert, n_used, rows = moe_plan(meta2, counts, MOE_ROW_TILE, MOE_TOKEN_TILE)
            xs = moe_dispatch(hp.reshape(b * t, d // 2), dest, rows, MOE_TOKEN_TILE)
            y = moe_experts(xs, tile_expert, n_used, moe_w_gate, moe_w_up, moe_w_down, j, MOE_ROW_TILE)
            x = moe_combine(x.reshape(b * t, d), g2, meta2, y, dest, t, MOE_TOKEN_TILE).reshape(b, t, d)
    return x
```

```python
import functools

import jax
import jax.numpy as jnp
from jax import lax
from jax.experimental import pallas as pl
from jax.experimental.pallas import tpu as pltpu

F32 = jnp.float32
BF16 = jnp.bfloat16

EPS = 1e-6
GLA_HEADS = 4
GLA_GATE_TAU = 16.0
GLA_CHUNK = 64
FOX_HEAD_DIM = 64
LOG2E = 1.4426950408889634
N_EXPERTS = 8
MOE_ROW_TILE = 1024
MOE_TOKEN_TILE = 512
LANES = 128
VMEM_LIMIT = 56 * 1024 * 1024


def _cparams(sem):
    return pltpu.CompilerParams(dimension_semantics=sem, vmem_limit_bytes=VMEM_LIMIT)


def _dot(a, b):
    return jnp.dot(a, b, preferred_element_type=F32)


def _dot_nt(a, b):
    return lax.dot_general(a, b, (((1,), (1,)), ((), ())), preferred_element_type=F32)


def _dot_tn(a, b):
    return lax.dot_general(a, b, (((0,), (0,)), ((), ())), preferred_element_type=F32)


def _split3(a):
    a1 = a.astype(BF16)
    r1 = a - a1.astype(F32)
    a2 = r1.astype(BF16)
    a3 = (r1 - a2.astype(F32)).astype(BF16)
    return a1, a2, a3


def _exact_left_dot(m01, a):
    a1, a2, a3 = _split3(a)
    return _dot(m01, a1) + _dot(m01, a2) + _dot(m01, a3)


def _log_sigmoid(x):
    return jnp.minimum(x, 0.0) - jnp.log1p(jnp.exp(-jnp.abs(x)))


def _sigmoid(x):
    return 1.0 / (1.0 + jnp.exp(-x))


def _silu(x):
    return x * _sigmoid(x)


def _norm_mod(x, gain, scale, shift):
    y = x * lax.rsqrt(jnp.mean(x * x, axis=-1, keepdims=True) + EPS)
    return (y * gain) * (1.0 + scale) + shift


def _ada_body(c_ref, w_ref, b_ref, o_ref):
    cond = _silu(c_ref[...]).astype(BF16)
    o_ref[0] = _dot(cond, w_ref[0].astype(BF16)) + b_ref[0]


def ada_mods(c_pad, ada_w, ada_b, tn=1536):
    depth, d, n = ada_w.shape
    rows = c_pad.shape[0]
    return pl.pallas_call(
        _ada_body,
        out_shape=jax.ShapeDtypeStruct((depth, rows, n), F32),
        grid=(depth, n // tn),
        in_specs=[
            pl.BlockSpec((rows, d), lambda i, j: (0, 0)),
            pl.BlockSpec((1, d, tn), lambda i, j: (i, 0, j)),
            pl.BlockSpec((1, 1, tn), lambda i, j: (i, 0, j)),
        ],
        out_specs=pl.BlockSpec((1, rows, tn), lambda i, j: (i, 0, j)),
        compiler_params=_cparams(("parallel", "parallel")),
        name="ada_mods",
    )(c_pad, ada_w, ada_b.reshape(depth, 1, n))


def _gla_in_body(x_ref, gain_ref, sc_ref, sh_ref, w_ref, wg_ref, bg_ref,
                 qk_ref, v_ref, r_ref, la_ref, *, dk, dv):
    h = _norm_mod(x_ref[0], gain_ref[...], sc_ref[0], sh_ref[0]).astype(BF16)
    qk_ref[0] = _dot(h, w_ref[:, 0:2 * dk])
    v_ref[0] = _dot(h, w_ref[:, 2 * dk:2 * dk + dv]).astype(BF16)
    r_ref[0] = _dot(h, w_ref[:, 2 * dk + dv:2 * dk + 2 * dv]).astype(BF16)
    g_low = _dot(h, w_ref[:, 2 * dk + 2 * dv:]).astype(BF16)
    gate = _dot(g_low, wg_ref[...]) + bg_ref[...]
    la_ref[0] = _log_sigmoid(gate) * (1.0 / GLA_GATE_TAU)


def gla_in_proj(x, gain, sc, sh, w_pad, wg_pad, b_gate, *, dk, dv, tm=512):
    b, t, d = x.shape
    n_pad = w_pad.shape[1]
    body = functools.partial(_gla_in_body, dk=dk, dv=dv)
    return pl.pallas_call(
        body,
        out_shape=(
            jax.ShapeDtypeStruct((b, t, 2 * dk), F32),
            jax.ShapeDtypeStruct((b, t, dv), BF16),
            jax.ShapeDtypeStruct((b, t, dv), BF16),
            jax.ShapeDtypeStruct((b, t, dk), F32),
        ),
        grid=(b, t // tm),
        in_specs=[
            pl.BlockSpec((1, tm, d), lambda i, j: (i, j, 0)),
            pl.BlockSpec((1, d), lambda i, j: (0, 0)),
            pl.BlockSpec((1, 1, d), lambda i, j: (i, 0, 0)),
            pl.BlockSpec((1, 1, d), lambda i, j: (i, 0, 0)),
            pl.BlockSpec((d, n_pad), lambda i, j: (0, 0)),
            pl.BlockSpec((LANES, dk), lambda i, j: (0, 0)),
            pl.BlockSpec((1, dk), lambda i, j: (0, 0)),
        ],
        out_specs=(
            pl.BlockSpec((1, tm, 2 * dk), lambda i, j: (i, j, 0)),
            pl.BlockSpec((1, tm, dv), lambda i, j: (i, j, 0)),
            pl.BlockSpec((1, tm, dv), lambda i, j: (i, j, 0)),
            pl.BlockSpec((1, tm, dk), lambda i, j: (i, j, 0)),
        ),
        compiler_params=_cparams(("parallel", "parallel")),
        name="gla_in_proj",
    )(x, gain, sc, sh, w_pad, wg_pad, b_gate)


def _gla_body(qk_ref, v_ref, la_ref, r_ref, on_ref, tri_ref, wo_ref, x_ref, g_ref, o_ref, st_ref,
              *, heads, hk, hv, n_chunks):
    @pl.when(pl.program_id(1) == 0)
    def _():
        st_ref[...] = jnp.zeros_like(st_ref)

    c = GLA_CHUNK
    dk = heads * hk
    tri = tri_ref[...]
    in_chunk_causal = tri > 0
    bcum = _exact_left_dot(tri, la_ref[0])
    b_last = [bcum[ci * c + c - 1:ci * c + c, :] for ci in range(n_chunks)]
    b_end = jnp.concatenate([jnp.broadcast_to(bl, (c, dk)) for bl in b_last], axis=0)
    q = qk_ref[0, :, 0:dk] * (hk ** -0.5)
    k = qk_ref[0, :, dk:2 * dk]
    q_dec = (q * jnp.exp(bcum)).astype(BF16)
    k_inv = (k * jnp.exp(-bcum)).astype(BF16)
    k_end = (k * jnp.exp(b_end - bcum)).astype(BF16)
    gated = []
    for hh in range(heads):
        ksl = slice(hh * hk, (hh + 1) * hk)
        vsl = slice(hh * hv, (hh + 1) * hv)
        v = v_ref[0, :, vsl]
        attn = jnp.where(in_chunk_causal, _dot_nt(q_dec[:, ksl], k_inv[:, ksl]), 0.0).astype(BF16)
        o_intra = _dot(attn, v)
        st = st_ref[hh]
        o_inter = []
        for ci in range(n_chunks):
            sl = slice(ci * c, (ci + 1) * c)
            o_inter.append(_dot_nt(q_dec[sl, ksl], st.astype(BF16)))
            st = jnp.exp(b_last[ci][:, ksl]) * st + _dot_tn(v[sl, :], k_end[sl, ksl])
        st_ref[hh] = st
        o = o_intra + jnp.concatenate(o_inter, axis=0)
        o_n = o * lax.rsqrt(jnp.mean(o * o, axis=-1, keepdims=True) + EPS) * on_ref[...]
        gated.append((o_n * _silu(r_ref[0, :, vsl].astype(F32))).astype(BF16))
    o_ref[0] = x_ref[0] + g_ref[0] * _dot(jnp.concatenate(gated, axis=-1), wo_ref[...])


def gla_mix(qk, v, r, la, o_norm, w_out, x, g, *, heads, tt=256):
    b, t, dk2 = qk.shape
    dk = dk2 // 2
    dv = v.shape[-1]
    d = x.shape[-1]
    hk, hv = dk // heads, dv // heads
    tri = jnp.kron(jnp.eye(tt // GLA_CHUNK, dtype=F32), jnp.tri(GLA_CHUNK, dtype=F32)).astype(BF16)
    body = functools.partial(_gla_body, heads=heads, hk=hk, hv=hv, n_chunks=tt // GLA_CHUNK)
    row = lambda i, j: (i, j, 0)
    return pl.pallas_call(
        body,
        out_shape=jax.ShapeDtypeStruct((b, t, d), F32),
        grid=(b, t // tt),
        in_specs=[
            pl.BlockSpec((1, tt, dk2), row),
            pl.BlockSpec((1, tt, dv), row),
            pl.BlockSpec((1, tt, dk), row),
            pl.BlockSpec((1, tt, dv), row),
            pl.BlockSpec((1, hv), lambda i, j: (0, 0)),
            pl.BlockSpec((tt, tt), lambda i, j: (0, 0)),
            pl.BlockSpec((dv, d), lambda i, j: (0, 0)),
            pl.BlockSpec((1, tt, d), row),
            pl.BlockSpec((1, 1, d), lambda i, j: (i, 0, 0)),
        ],
        out_specs=pl.BlockSpec((1, tt, d), row),
        scratch_shapes=[pltpu.VMEM((heads, hv, hk), F32)],
        compiler_params=_cparams(("parallel", "arbitrary")),
        name="gla_mix",
    )(qk, v, la, r, o_norm, tri, w_out, x, g)


def _ffn_body(x_ref, gain_ref, sc_ref, sh_ref, g_ref, wg_ref, wu_ref, wd_ref, o_ref, h_ref, acc_ref):
    f = pl.program_id(2)

    @pl.when(f == 0)
    def _():
        h_ref[...] = _norm_mod(x_ref[0], gain_ref[...], sc_ref[0], sh_ref[0]).astype(BF16)
        acc_ref[...] = jnp.zeros_like(acc_ref)

    h = h_ref[...]
    a = (_silu(_dot(h, wg_ref[...])) * _dot(h, wu_ref[...])).astype(BF16)
    acc_ref[...] += _dot(a, wd_ref[...])

    @pl.when(f == pl.num_programs(2) - 1)
    def _():
        o_ref[0] = x_ref[0] + g_ref[0] * acc_ref[...]


def ffn_dense(x, gain, sc, sh, g, w_gate, w_up, w_down, tm=1024, tf=512):
    b, t, d = x.shape
    ff = w_gate.shape[1]
    return pl.pallas_call(
        _ffn_body,
        out_shape=jax.ShapeDtypeStruct((b, t, d), F32),
        grid=(b, t // tm, ff // tf),
        in_specs=[
            pl.BlockSpec((1, tm, d), lambda i, j, f: (i, j, 0)),
            pl.BlockSpec((1, d), lambda i, j, f: (0, 0)),
            pl.BlockSpec((1, 1, d), lambda i, j, f: (i, 0, 0)),
            pl.BlockSpec((1, 1, d), lambda i, j, f: (i, 0, 0)),
            pl.BlockSpec((1, 1, d), lambda i, j, f: (i, 0, 0)),
            pl.BlockSpec((d, tf), lambda i, j, f: (0, f)),
            pl.BlockSpec((d, tf), lambda i, j, f: (0, f)),
            pl.BlockSpec((tf, d), lambda i, j, f: (f, 0)),
        ],
        out_specs=pl.BlockSpec((1, tm, d), lambda i, j, f: (i, j, 0)),
        scratch_shapes=[pltpu.VMEM((tm, d), BF16), pltpu.VMEM((tm, d), F32)],
        compiler_params=_cparams(("parallel", "parallel", "arbitrary")),
        name="ffn_dense",
    )(x, gain, sc, sh, g, w_gate, w_up, w_down)


def _group_rms(x, gmean):
    sq = x * x
    hi = sq.astype(BF16)
    lo = (sq - hi.astype(F32)).astype(BF16)
    cols = gmean.shape[0]
    parts = []
    for s in range(0, x.shape[-1], cols):
        parts.append(_dot(hi[:, s:s + cols], gmean) + _dot(lo[:, s:s + cols], gmean))
    ms = jnp.concatenate(parts, axis=-1)
    return x * lax.rsqrt(ms + EPS)


def _fox_in_body(x_ref, gain_ref, sc_ref, sh_ref, w_ref, bf_ref, qn_ref, kn_ref, gm_ref, tri_ref, pf_ref,
                 q_ref, k_ref, kf_ref, v_ref, sg_ref, carry_ref, *, d, heads):
    @pl.when(pl.program_id(1) == 0)
    def _():
        carry_ref[...] = jnp.zeros_like(carry_ref)

    h = _norm_mod(x_ref[0], gain_ref[...], sc_ref[0], sh_ref[0]).astype(BF16)
    gm = gm_ref[...]
    q = _group_rms(_dot(h, w_ref[:, 0:d]), gm) * qn_ref[...]
    q_ref[0] = (q * (FOX_HEAD_DIM ** -0.5 * LOG2E)).astype(BF16)
    k = _group_rms(_dot(h, w_ref[:, d:2 * d]), gm) * kn_ref[...]
    k_ref[0] = k.astype(BF16)
    v_ref[0, 0] = _dot(h, w_ref[:, 2 * d:3 * d]).T.astype(BF16)
    sg_ref[0] = _sigmoid(_dot(h, w_ref[:, 3 * d:4 * d])).astype(BF16)
    log_f = _log_sigmoid(_dot(h, w_ref[:, 4 * d:]) + bf_ref[...])
    fcum = _exact_left_dot(tri_ref[...], log_f) + carry_ref[...]
    carry_ref[...] = fcum[fcum.shape[0] - 1:, :]
    lane = lax.broadcasted_iota(jnp.int32, fcum.shape, 1)
    p1, p2, p3 = _split3(jnp.where(lane < heads, fcum * (-LOG2E), 0.0))
    packed = (p1.astype(F32) + pltpu.roll(p2.astype(F32), heads, 1)
              + pltpu.roll(p3.astype(F32), 2 * heads, 1)).astype(BF16)
    kf_ref[0] = _dot(packed, pf_ref[...]).astype(BF16)


def _fox_bias_placement(heads, d):
    src, dst = [], []
    for i in range(3):
        for hd in range(heads):
            src.append(i * heads + hd)
            dst.append((hd // 2) * LANES + 3 * (hd % 2) + i)
    return jnp.zeros((LANES, d), F32).at[jnp.array(src), jnp.array(dst)].set(1.0).astype(BF16)


def fox_in_proj(x, gain, sc, sh, w_pad, bf_pad, qn_t, kn_t, heads, tm=512):
    b, t, d = x.shape
    n_pad = w_pad.shape[1]
    gcols = 2 * LANES
    gm = (jnp.kron(jnp.eye(gcols // FOX_HEAD_DIM, dtype=F32),
                   jnp.ones((FOX_HEAD_DIM, FOX_HEAD_DIM), F32)) / FOX_HEAD_DIM).astype(BF16)
    tri = jnp.tri(tm, dtype=BF16)
    pf = _fox_bias_placement(heads, d)
    body = functools.partial(_fox_in_body, d=d, heads=heads)
    bf = lambda s: jax.ShapeDtypeStruct(s, BF16)
    row = lambda i, j: (i, j, 0)
    const = lambda i, j: (0, 0)
    return pl.pallas_call(
        body,
        out_shape=(bf((b, t, d)), bf((b, t, d)), bf((b, t, d)), bf((b, t // tm, d, tm)), bf((b, t, d))),
        grid=(b, t // tm),
        in_specs=[
            pl.BlockSpec((1, tm, d), row),
            pl.BlockSpec((1, d), const),
            pl.BlockSpec((1, 1, d), lambda i, j: (i, 0, 0)),
            pl.BlockSpec((1, 1, d), lambda i, j: (i, 0, 0)),
            pl.BlockSpec((d, n_pad), const),
            pl.BlockSpec((1, LANES), const),
            pl.BlockSpec((1, d), const),
            pl.BlockSpec((1, d), const),
            pl.BlockSpec((gcols, gcols), const),
            pl.BlockSpec((tm, tm), const),
            pl.BlockSpec((LANES, d), const),
        ],
        out_specs=(pl.BlockSpec((1, tm, d), row), pl.BlockSpec((1, tm, d), row), pl.BlockSpec((1, tm, d), row),
                   pl.BlockSpec((1, 1, d, tm), lambda i, j: (i, j, 0, 0)), pl.BlockSpec((1, tm, d), row)),
        scratch_shapes=[pltpu.VMEM((1, LANES), F32)],
        compiler_params=_cparams(("parallel", "arbitrary")),
        name="fox_in_proj",
    )(x, gain, sc, sh, w_pad, bf_pad, qn_t, kn_t, gm, tri, pf)


def _fox_attn_body(q_ref, k_ref, kf_ref, vt_ref, sg_ref, o_ref, s0_ref, s1_ref, mx0_ref, mx1_ref,
                   m_ref, l_ref, acc_ref, *, tq, tk):
    i = pl.program_id(2)
    hd = FOX_HEAD_DIM
    lane = lax.broadcasted_iota(jnp.int32, (1, LANES), 1)
    q2 = q_ref[0]
    qs = []
    for hh in range(2):
        qh = jnp.where((lane >= hh * hd) & (lane < (hh + 1) * hd), q2, jnp.zeros_like(q2))
        ones = jnp.where((lane >= 3 * hh) & (lane < 3 * hh + 3), 1.0, 0.0).astype(BF16)
        qs.append(jnp.concatenate([qh, jnp.broadcast_to(ones, qh.shape)], axis=-1))

    m_ref[...] = jnp.full_like(m_ref, -jnp.inf)
    l_ref[...] = jnp.zeros_like(l_ref)
    acc_ref[...] = jnp.zeros_like(acc_ref)

    def scores(j, s_ref, mx_ref, diagonal):
        ks = pl.multiple_of(j * tk, tk)
        kcat = jnp.concatenate([k_ref[0, pl.ds(ks, tk), :], kf_ref[0, pl.ds(ks, tk), :]], axis=-1)
        for hh in range(2):
            st = _dot_nt(kcat, qs[hh])
            if diagonal:
                key = lax.broadcasted_iota(jnp.int32, (tk, tq), 0)
                qry = lax.broadcasted_iota(jnp.int32, (tk, tq), 1)
                st = jnp.where(key <= qry, st, -jnp.inf)
            s_ref[hh] = st
            mx_ref[hh] = jnp.max(st, axis=0, keepdims=True)

    def consume(j, s_ref, mx_ref):
        vt = vt_ref[0, j]
        for hh in range(2):
            m_prev = m_ref[hh]
            m_new = jnp.maximum(m_prev, mx_ref[hh])
            alpha = jnp.exp2(m_prev - m_new)
            p = jnp.exp2(s_ref[hh] - m_new)
            l_ref[hh] = alpha * l_ref[hh] + jnp.sum(p, axis=0, keepdims=True)
            acc_ref[hh] = alpha * acc_ref[hh] + _dot(vt, p.astype(BF16))
            m_ref[hh] = m_new

    scores(i, s0_ref, mx0_ref, True)

    def pair(u, carry):
        t = 2 * u
        scores(t, s1_ref, mx1_ref, False)
        consume(jnp.where(u == 0, i, t - 1), s0_ref, mx0_ref)
        scores(t + 1, s0_ref, mx0_ref, False)
        consume(t, s1_ref, mx1_ref)
        return carry

    lax.fori_loop(0, i // 2, pair, 0)

    @pl.when(i % 2 == 1)
    def _():
        scores(i - 1, s1_ref, mx1_ref, False)
        consume(jnp.where(i == 1, i, i - 2), s0_ref, mx0_ref)
        consume(i - 1, s1_ref, mx1_ref)

    @pl.when(i % 2 == 0)
    def _():
        consume(jnp.maximum(i - 1, 0), s0_ref, mx0_ref)

    chan = lax.broadcasted_iota(jnp.int32, (2 * hd, 1), 0)
    o_t = jnp.where(chan < hd, acc_ref[0] * (1.0 / l_ref[0]), acc_ref[1] * (1.0 / l_ref[1]))
    o_ref[0] = (o_t.T * sg_ref[0].astype(F32)).astype(BF16)


def fox_attention(q, k, kf, vt, sg, tq=512):
    b, t, d = q.shape
    pw = 2 * FOX_HEAD_DIM
    pairs = d // pw
    nk, tk = vt.shape[1], vt.shape[3]
    assert tk == tq
    body = functools.partial(_fox_attn_body, tq=tq, tk=tk)
    qmap = lambda bi, p, i: (bi, i, p)
    kmap = lambda bi, p, i: (bi, 0, p)
    return pl.pallas_call(
        body,
        out_shape=jax.ShapeDtypeStruct((b, t, d), BF16),
        grid=(b, pairs, t // tq),
        in_specs=[
            pl.BlockSpec((1, tq, pw), qmap),
            pl.BlockSpec((1, t, pw), kmap),
            pl.BlockSpec((1, t, pw), kmap),
            pl.BlockSpec((1, nk, pw, tk), lambda bi, p, i: (bi, 0, p, 0)),
            pl.BlockSpec((1, tq, pw), qmap),
        ],
        out_specs=pl.BlockSpec((1, tq, pw), qmap),
        scratch_shapes=[
            pltpu.VMEM((2, tk, tq), F32),
            pltpu.VMEM((2, tk, tq), F32),
            pltpu.VMEM((2, 1, tq), F32),
            pltpu.VMEM((2, 1, tq), F32),
            pltpu.VMEM((2, 1, tq), F32),
            pltpu.VMEM((2, 1, tq), F32),
            pltpu.VMEM((2, pw, tq), F32),
        ],
        compiler_params=_cparams(("parallel", "parallel", "arbitrary")),
        name="fox_attention",
    )(q, k, kf, vt, sg)


META_E1, META_E2, META_W1, META_W2, META_R1, META_R2 = range(6)


def _lane_pick(a, lane, idx):
    return jnp.sum(jnp.where(lane == idx, a, 0.0), axis=-1, keepdims=True)


def _router_body(a_ref, wo_ref, g1_ref, x_ref, gain_ref, sc_ref, sh_ref, wr_ref, br_ref, tri_ref,
                 xo_ref, hp_ref, meta_ref, cnt_ref, carry_ref):
    @pl.when((pl.program_id(0) == 0) & (pl.program_id(1) == 0))
    def _():
        carry_ref[...] = jnp.zeros_like(carry_ref)

    x = x_ref[0] + g1_ref[0] * _dot(a_ref[0], wo_ref[...])
    xo_ref[0] = x
    h = _norm_mod(x, gain_ref[...], sc_ref[0], sh_ref[0])
    half = h.shape[-1] // 2
    hb = h.astype(BF16).astype(F32)
    lo = lax.bitcast_convert_type(hb[:, :half], jnp.uint32) >> 16
    hi = lax.bitcast_convert_type(hb[:, half:], jnp.uint32) & jnp.uint32(0xFFFF0000)
    hp_ref[0] = hi | lo

    h1 = h.astype(BF16)
    h2 = (h - h1.astype(F32)).astype(BF16)
    wr = wr_ref[...]
    w1 = wr.astype(BF16)
    w2 = (wr - w1.astype(F32)).astype(BF16)
    logits = _dot(h1, w1) + _dot(h1, w2) + _dot(h2, w1) + br_ref[...]
    lane = lax.broadcasted_iota(jnp.int32, logits.shape, 1)
    logits = jnp.where(lane < N_EXPERTS, logits, -jnp.inf)
    m1 = jnp.max(logits, axis=-1, keepdims=True)
    i1 = jnp.min(jnp.where(logits == m1, lane, LANES), axis=-1, keepdims=True)
    rest = jnp.where(lane == i1, -jnp.inf, logits)
    m2 = jnp.max(rest, axis=-1, keepdims=True)
    i2 = jnp.min(jnp.where(rest == m2, lane, LANES), axis=-1, keepdims=True)
    e2 = jnp.exp(m2 - m1)
    w1 = 1.0 / (1.0 + e2)
    w2 = e2 / (1.0 + e2)

    onehot = jnp.where(lane == i1, 1.0, 0.0) + jnp.where(lane == i2, 1.0, 0.0)
    before = _dot(tri_ref[...], onehot.astype(BF16)) + carry_ref[...]
    r1 = _lane_pick(before, lane, i1)
    r2 = _lane_pick(before, lane, i2)
    last = onehot.shape[0] - 1
    total = before[last:, :] + onehot[last:, :]
    carry_ref[...] = total
    cnt_ref[...] = total

    meta = jnp.zeros_like(logits)
    for slot, val in ((META_E1, i1.astype(F32)), (META_E2, i2.astype(F32)), (META_W1, w1), (META_W2, w2),
                      (META_R1, r1), (META_R2, r2)):
        meta = jnp.where(lane == slot, val, meta)
    meta_ref[0] = meta


def moe_router(a, w_out, g1, x, gain, sc, sh, wr_pad, br_pad, tm=512):
    b, t, d = x.shape
    row = lambda i, j: (i, j, 0)
    const = lambda i, j: (0, 0)
    tri = jnp.tri(tm, k=-1, dtype=BF16)
    return pl.pallas_call(
        _router_body,
        out_shape=(jax.ShapeDtypeStruct((b, t, d), F32),
                   jax.ShapeDtypeStruct((b, t, d // 2), jnp.uint32),
                   jax.ShapeDtypeStruct((b, t, LANES), F32),
                   jax.ShapeDtypeStruct((1, LANES), F32)),
        grid=(b, t // tm),
        in_specs=[
            pl.BlockSpec((1, tm, a.shape[-1]), row),
            pl.BlockSpec(w_out.shape, const),
            pl.BlockSpec((1, 1, d), lambda i, j: (i, 0, 0)),
            pl.BlockSpec((1, tm, d), row),
            pl.BlockSpec((1, d), const),
            pl.BlockSpec((1, 1, d), lambda i, j: (i, 0, 0)),
            pl.BlockSpec((1, 1, d), lambda i, j: (i, 0, 0)),
            pl.BlockSpec((d, LANES), const),
            pl.BlockSpec((1, LANES), const),
            pl.BlockSpec((tm, tm), const),
        ],
        out_specs=(pl.BlockSpec((1, tm, d), row), pl.BlockSpec((1, tm, d // 2), row),
                   pl.BlockSpec((1, tm, LANES), row), pl.BlockSpec((1, LANES), const)),
        scratch_shapes=[pltpu.VMEM((1, LANES), F32)],
        compiler_params=_cparams(("arbitrary", "arbitrary")),
        name="moe_router",
    )(a, w_out, g1, x, gain, sc, sh, wr_pad, br_pad, tri)


def _row_copy(src_ref, src_row, dst_ref, dst_row, sem):
    return pltpu.make_async_copy(src_ref.at[pl.ds(src_row, 1), :], dst_ref.at[pl.ds(dst_row, 1), :], sem)


def _dispatch_body(dest_ref, hp_ref, xs_in_ref, xs_ref, sem, *, tm):
    del xs_in_ref

    for r in range(tm):
        _row_copy(hp_ref, r, xs_ref, dest_ref[0, 0, r], sem).start(priority=0)
        _row_copy(hp_ref, r, xs_ref, dest_ref[0, 0, tm + r], sem).start(priority=1)

    def drain(r, carry):
        _row_copy(hp_ref, r, xs_ref, dest_ref[0, 0, r], sem).wait()
        _row_copy(hp_ref, r, xs_ref, dest_ref[0, 0, tm + r], sem).wait()
        return carry

    lax.fori_loop(0, tm, drain, 0, unroll=8)


def moe_dispatch(hp, dest, rows, tm):
    n, half = hp.shape
    body = functools.partial(_dispatch_body, tm=tm)
    return pl.pallas_call(
        body,
        out_shape=jax.ShapeDtypeStruct((rows, half), jnp.uint32),
        grid=(n // tm,),
        in_specs=[
            pl.BlockSpec((1, 1, 2 * tm), lambda i: (i, 0, 0), memory_space=pltpu.SMEM),
            pl.BlockSpec((tm, half), lambda i: (i, 0)),
            pl.BlockSpec(memory_space=pl.ANY),
        ],
        out_specs=pl.BlockSpec(memory_space=pl.ANY),
        scratch_shapes=[pltpu.SemaphoreType.DMA],
        input_output_aliases={2: 0},
        compiler_params=pltpu.CompilerParams(dimension_semantics=("arbitrary",), vmem_limit_bytes=VMEM_LIMIT,
                                             disable_bounds_checks=True),
        name="moe_dispatch",
    )(dest, hp, jnp.zeros((rows, half), jnp.uint32))


def _experts_body(te_ref, nu_ref, xs_ref, wg_ref, wu_ref, wd_ref, y_ref, h_ref, acc_ref):
    del te_ref
    i = pl.program_id(0)
    f = pl.program_id(1)

    @pl.when(i < nu_ref[0])
    def _():
        @pl.when(f == 0)
        def _():
            xp = xs_ref[...]
            lo = lax.bitcast_convert_type(xp << 16, F32).astype(BF16)
            hi = lax.bitcast_convert_type(xp & jnp.uint32(0xFFFF0000), F32).astype(BF16)
            h_ref[...] = jnp.concatenate([lo, hi], axis=-1)
            acc_ref[...] = jnp.zeros_like(acc_ref)

        h = h_ref[...]
        a = (_silu(_dot(h, wg_ref[0, 0].astype(BF16))) * _dot(h, wu_ref[0, 0].astype(BF16))).astype(BF16)
        acc_ref[...] += _dot(a, wd_ref[0, 0].astype(BF16))

        @pl.when(f == pl.num_programs(1) - 1)
        def _():
            y_ref[...] = acc_ref[...]

    @pl.when((i >= nu_ref[0]) & (f == pl.num_programs(1) - 1))
    def _():
        y_ref[...] = jnp.zeros_like(y_ref)


def moe_experts(xs, tile_expert, n_used, w_gate, w_up, w_down, layer, tr, tf=512):
    rows, half = xs.shape
    _, ne, d, ff = w_gate.shape
    nf = ff // tf
    tile = lambda i, nu: jnp.maximum(jnp.minimum(i, nu[0] - 1), 0)
    fblk = lambda i, f, nu: jnp.where(i < nu[0], f, nf - 1)
    return pl.pallas_call(
        _experts_body,
        out_shape=jax.ShapeDtypeStruct((rows, d), F32),
        grid_spec=pltpu.PrefetchScalarGridSpec(
            num_scalar_prefetch=2,
            grid=(rows // tr, nf),
            in_specs=[
                pl.BlockSpec((tr, half), lambda i, f, te, nu: (tile(i, nu), 0)),
                pl.BlockSpec((1, 1, d, tf), lambda i, f, te, nu: (layer, te[tile(i, nu)], 0, fblk(i, f, nu))),
                pl.BlockSpec((1, 1, d, tf), lambda i, f, te, nu: (layer, te[tile(i, nu)], 0, fblk(i, f, nu))),
                pl.BlockSpec((1, 1, tf, d), lambda i, f, te, nu: (layer, te[tile(i, nu)], fblk(i, f, nu), 0)),
            ],
            out_specs=pl.BlockSpec((tr, d), lambda i, f, te, nu: (i, 0)),
            scratch_shapes=[pltpu.VMEM((tr, d), BF16), pltpu.VMEM((tr, d), F32)],
        ),
        compiler_params=_cparams(("arbitrary", "arbitrary")),
        name="moe_experts",
    )(tile_expert, n_used, xs, w_gate, w_up, w_down)


def _combine_body(dest_ref, x_ref, g_ref, meta_ref, y_ref, o_ref, buf_ref, sem, *, tm):
    for r in range(tm):
        _row_copy(y_ref, dest_ref[0, 0, r], buf_ref.at[0], r, sem).start(priority=0)
        _row_copy(y_ref, dest_ref[0, 0, tm + r], buf_ref.at[1], r, sem).start(priority=1)

    def drain(r, carry):
        _row_copy(y_ref, dest_ref[0, 0, r], buf_ref.at[0], r, sem).wait()
        _row_copy(y_ref, dest_ref[0, 0, tm + r], buf_ref.at[1], r, sem).wait()
        return carry

    lax.fori_loop(0, tm, drain, 0, unroll=8)

    meta = meta_ref[...]
    lane = lax.broadcasted_iota(jnp.int32, meta.shape, 1)
    w1 = _lane_pick(meta, lane, META_W1)
    w2 = _lane_pick(meta, lane, META_W2)
    o_ref[...] = x_ref[...] + g_ref[0] * (w1 * buf_ref[0] + w2 * buf_ref[1])


def moe_combine(x2, g, meta2, y, dest, t, tm):
    n, d = x2.shape
    per_batch = t // tm
    body = functools.partial(_combine_body, tm=tm)
    return pl.pallas_call(
        body,
        out_shape=jax.ShapeDtypeStruct((n, d), F32),
        grid=(n // tm,),
        in_specs=[
            pl.BlockSpec((1, 1, 2 * tm), lambda i: (i, 0, 0), memory_space=pltpu.SMEM),
            pl.BlockSpec((tm, d), lambda i: (i, 0)),
            pl.BlockSpec((1, 1, d), lambda i: (i // per_batch, 0, 0)),
            pl.BlockSpec((tm, LANES), lambda i: (i, 0)),
            pl.BlockSpec(memory_space=pl.ANY),
        ],
        out_specs=pl.BlockSpec((tm, d), lambda i: (i, 0)),
        scratch_shapes=[pltpu.VMEM((2, tm, d), F32), pltpu.SemaphoreType.DMA],
        compiler_params=pltpu.CompilerParams(dimension_semantics=("arbitrary",), vmem_limit_bytes=VMEM_LIMIT,
                                             disable_bounds_checks=True),
        name="moe_combine",
    )(dest, x2, g, meta2, y)


def moe_plan(meta2, counts, tr, tm):
    n = meta2.shape[0]
    e1 = meta2[:, META_E1].astype(jnp.int32)
    e2 = meta2[:, META_E2].astype(jnp.int32)
    cnt = counts[0, :N_EXPERTS].astype(jnp.int32)
    padded = (cnt + tr - 1) // tr * tr
    ends = jnp.cumsum(padded)
    starts = ends - padded
    d1 = starts[e1] + meta2[:, META_R1].astype(jnp.int32)
    d2 = starts[e2] + meta2[:, META_R2].astype(jnp.int32)
    dest = jnp.concatenate([d1.reshape(n // tm, 1, tm), d2.reshape(n // tm, 1, tm)], axis=-1)
    n_tiles = 2 * n // tr + N_EXPERTS
    tile_start = jnp.arange(n_tiles, dtype=jnp.int32) * tr
    tile_expert = jnp.sum((tile_start[:, None] >= ends[None, :]).astype(jnp.int32), axis=1)
    tile_expert = jnp.minimum(tile_expert, N_EXPERTS - 1)
    n_used = (ends[-1:] // tr).astype(jnp.int32)
    return dest, tile_expert, n_used, n_tiles * tr


def _pad_cols(w, n):
    return jnp.pad(w, ((0, 0), (0, n - w.shape[1])))


def kernel(x, c, ada_w, ada_b, norm_gain, gla_w_in, gla_w_gate_up, gla_b_gate, gla_o_norm, gla_w_out,
           fox_w_in, fox_b_f, fox_q_norm, fox_k_norm, fox_w_out, ffn_w_gate, ffn_w_up, ffn_w_down,
           moe_w_router, moe_b_router, moe_w_gate, moe_w_up, moe_w_down):
    b, t, d = x.shape
    depth = ada_w.shape[0]
    gla_rank = gla_w_gate_up.shape[1]
    dk = gla_w_gate_up.shape[2]
    dv = gla_w_out.shape[1]
    fox_heads = fox_b_f.shape[1]

    c_pad = jnp.pad(c, ((0, 8 - b), (0, 0)))
    mods = ada_mods(c_pad, ada_w, ada_b)[:, :b]
    mods = mods.reshape(depth, b, 6, 1, d)

    for i in range(depth):
        j = i // 2
        sh1, sc1, g1, sh2, sc2, g2 = (mods[i, :, m] for m in range(6))
        gain1 = norm_gain[i, 0].reshape(1, d)
        gain2 = norm_gain[i, 1].reshape(1, d)
        if i % 2 == 0:
            n_main = 2 * dk + 2 * dv
            w_pad = _pad_cols(gla_w_in[j], n_main + LANES).astype(BF16)
            wg_pad = jnp.pad(gla_w_gate_up[j], ((0, LANES - gla_rank), (0, 0))).astype(BF16)
            qk, v, r, la = gla_in_proj(x, gain1, sc1, sh1, w_pad, wg_pad, gla_b_gate[j].reshape(1, dk),
                                       dk=dk, dv=dv)
            x = gla_mix(qk, v, r, la, gla_o_norm[j].reshape(1, -1), gla_w_out[j].astype(BF16), x, g1,
                        heads=GLA_HEADS)
            x = ffn_dense(x, gain2, sc2, sh2, g2, ffn_w_gate[j].astype(BF16), ffn_w_up[j].astype(BF16),
                          ffn_w_down[j].astype(BF16))
        else:
            w_pad = _pad_cols(fox_w_in[j], 4 * d + LANES).astype(BF16)
            bf_pad = jnp.pad(fox_b_f[j], (0, LANES - fox_heads)).reshape(1, LANES)
            qn_t = jnp.tile(fox_q_norm[j], fox_heads).reshape(1, d)
            kn_t = jnp.tile(fox_k_norm[j], fox_heads).reshape(1, d)
            q, k, kf, v, sg = fox_in_proj(x, gain1, sc1, sh1, w_pad, bf_pad, qn_t, kn_t, fox_heads)
            a = fox_attention(q, k, kf, v, sg)
            wr_pad = _pad_cols(moe_w_router[j], LANES)
            br_pad = jnp.pad(moe_b_router[j], (0, LANES - N_EXPERTS)).reshape(1, LANES)
            x, hp, meta, counts = moe_router(a, fox_w_out[j].astype(BF16), g1, x, gain2, sc2, sh2,
                                             wr_pad, br_pad)
            meta2 = meta.reshape(b * t, LANES)
            dest, tile_expert, n_used, rows = moe_plan(meta2, counts, MOE_ROW_TILE, MOE_TOKEN_TILE)
            xs = moe_dispatch(hp.reshape(b * t, d // 2), dest, rows, MOE_TOKEN_TILE)
            y = moe_experts(xs, tile_expert, n_used, moe_w_gate, moe_w_up, moe_w_down, j, MOE_ROW_TILE)
            x = moe_combine(x.reshape(b * t, d), g2, meta2, y, dest, t, MOE_TOKEN_TILE).reshape(b, t, d)
    return x
```

```python
import functools

import jax
import jax.numpy as jnp
from jax import lax
from jax.experimental import pallas as pl
from jax.experimental.pallas import tpu as pltpu

F32 = jnp.float32
BF16 = jnp.bfloat16

EPS = 1e-6
GLA_HEADS = 4
GLA_GATE_TAU = 16.0
GLA_CHUNK = 64
FOX_HEAD_DIM = 64
LOG2E = 1.4426950408889634
N_EXPERTS = 8
MOE_ROW_TILE = 1024
MOE_TOKEN_TILE = 512
LANES = 128
VMEM_LIMIT = 56 * 1024 * 1024


def _cparams(sem):
    return pltpu.CompilerParams(dimension_semantics=sem, vmem_limit_bytes=VMEM_LIMIT)


def _dot(a, b):
    return jnp.dot(a, b, preferred_element_type=F32)


def _dot_nt(a, b):
    return lax.dot_general(a, b, (((1,), (1,)), ((), ())), preferred_element_type=F32)


def _dot_tn(a, b):
    return lax.dot_general(a, b, (((0,), (0,)), ((), ())), preferred_element_type=F32)


def _split3(a):
    a1 = a.astype(BF16)
    r1 = a - a1.astype(F32)
    a2 = r1.astype(BF16)
    a3 = (r1 - a2.astype(F32)).astype(BF16)
    return a1, a2, a3


def _exact_left_dot(m01, a):
    a1, a2, a3 = _split3(a)
    return _dot(m01, a1) + _dot(m01, a2) + _dot(m01, a3)


def _log_sigmoid(x):
    return jnp.minimum(x, 0.0) - jnp.log1p(jnp.exp(-jnp.abs(x)))


def _sigmoid(x):
    return 1.0 / (1.0 + jnp.exp(-x))


def _silu(x):
    return x * _sigmoid(x)


def _norm_mod(x, gain, scale, shift):
    y = x * lax.rsqrt(jnp.mean(x * x, axis=-1, keepdims=True) + EPS)
    return (y * gain) * (1.0 + scale) + shift


def _ada_body(c_ref, w_ref, b_ref, o_ref):
    cond = _silu(c_ref[...]).astype(BF16)
    o_ref[0] = _dot(cond, w_ref[0].astype(BF16)) + b_ref[0]


def ada_mods(c_pad, ada_w, ada_b, tn=1536):
    depth, d, n = ada_w.shape
    rows = c_pad.shape[0]
    return pl.pallas_call(
        _ada_body,
        out_shape=jax.ShapeDtypeStruct((depth, rows, n), F32),
        grid=(depth, n // tn),
        in_specs=[
            pl.BlockSpec((rows, d), lambda i, j: (0, 0)),
            pl.BlockSpec((1, d, tn), lambda i, j: (i, 0, j)),
            pl.BlockSpec((1, 1, tn), lambda i, j: (i, 0, j)),
        ],
        out_specs=pl.BlockSpec((1, rows, tn), lambda i, j: (i, 0, j)),
        compiler_params=_cparams(("parallel", "parallel")),
        name="ada_mods",
    )(c_pad, ada_w, ada_b.reshape(depth, 1, n))


def _gla_in_body(x_ref, gain_ref, sc_ref, sh_ref, w_ref, wg_ref, bg_ref,
                 qk_ref, v_ref, r_ref, la_ref, *, dk, dv):
    h = _norm_mod(x_ref[0], gain_ref[...], sc_ref[0], sh_ref[0]).astype(BF16)
    qk_ref[0] = _dot(h, w_ref[:, 0:2 * dk])
    v_ref[0] = _dot(h, w_ref[:, 2 * dk:2 * dk + dv]).astype(BF16)
    r_ref[0] = _dot(h, w_ref[:, 2 * dk + dv:2 * dk + 2 * dv]).astype(BF16)
    g_low = _dot(h, w_ref[:, 2 * dk + 2 * dv:]).astype(BF16)
    gate = _dot(g_low, wg_ref[...]) + bg_ref[...]
    la_ref[0] = _log_sigmoid(gate) * (1.0 / GLA_GATE_TAU)


def gla_in_proj(x, gain, sc, sh, w_pad, wg_pad, b_gate, *, dk, dv, tm=512):
    b, t, d = x.shape
    n_pad = w_pad.shape[1]
    body = functools.partial(_gla_in_body, dk=dk, dv=dv)
    return pl.pallas_call(
        body,
        out_shape=(
            jax.ShapeDtypeStruct((b, t, 2 * dk), F32),
            jax.ShapeDtypeStruct((b, t, dv), BF16),
            jax.ShapeDtypeStruct((b, t, dv), BF16),
            jax.ShapeDtypeStruct((b, t, dk), F32),
        ),
        grid=(b, t // tm),
        in_specs=[
            pl.BlockSpec((1, tm, d), lambda i, j: (i, j, 0)),
            pl.BlockSpec((1, d), lambda i, j: (0, 0)),
            pl.BlockSpec((1, 1, d), lambda i, j: (i, 0, 0)),
            pl.BlockSpec((1, 1, d), lambda i, j: (i, 0, 0)),
            pl.BlockSpec((d, n_pad), lambda i, j: (0, 0)),
            pl.BlockSpec((LANES, dk), lambda i, j: (0, 0)),
            pl.BlockSpec((1, dk), lambda i, j: (0, 0)),
        ],
        out_specs=(
            pl.BlockSpec((1, tm, 2 * dk), lambda i, j: (i, j, 0)),
            pl.BlockSpec((1, tm, dv), lambda i, j: (i, j, 0)),
            pl.BlockSpec((1, tm, dv), lambda i, j: (i, j, 0)),
            pl.BlockSpec((1, tm, dk), lambda i, j: (i, j, 0)),
        ),
        compiler_params=_cparams(("parallel", "parallel")),
        name="gla_in_proj",
    )(x, gain, sc, sh, w_pad, wg_pad, b_gate)


def _gla_body(qk_ref, v_ref, la_ref, r_ref, on_ref, tri_ref, wo_ref, x_ref, g_ref, o_ref, st_ref,
              *, heads, hk, hv, n_chunks):
    @pl.when(pl.program_id(1) == 0)
    def _():
        st_ref[...] = jnp.zeros_like(st_ref)

    c = GLA_CHUNK
    dk = heads * hk
    tri = tri_ref[...]
    in_chunk_causal = tri > 0
    bcum = _exact_left_dot(tri, la_ref[0])
    b_last = [bcum[ci * c + c - 1:ci * c + c, :] for ci in range(n_chunks)]
    b_end = jnp.concatenate([jnp.broadcast_to(bl, (c, dk)) for bl in b_last], axis=0)
    q = qk_ref[0, :, 0:dk] * (hk ** -0.5)
    k = qk_ref[0, :, dk:2 * dk]
    q_dec = (q * jnp.exp(bcum)).astype(BF16)
    k_inv = (k * jnp.exp(-bcum)).astype(BF16)
    k_end = (k * jnp.exp(b_end - bcum)).astype(BF16)
    gated = []
    for hh in range(heads):
        ksl = slice(hh * hk, (hh + 1) * hk)
        vsl = slice(hh * hv, (hh + 1) * hv)
        v = v_ref[0, :, vsl]
        attn = jnp.where(in_chunk_causal, _dot_nt(q_dec[:, ksl], k_inv[:, ksl]), 0.0).astype(BF16)
        o_intra = _dot(attn, v)
        st = st_ref[hh]
        o_inter = []
        for ci in range(n_chunks):
            sl = slice(ci * c, (ci + 1) * c)
            o_inter.append(_dot_nt(q_dec[sl, ksl], st.astype(BF16)))
            st = jnp.exp(b_last[ci][:, ksl]) * st + _dot_tn(v[sl, :], k_end[sl, ksl])
        st_ref[hh] = st
        o = o_intra + jnp.concatenate(o_inter, axis=0)
        o_n = o * lax.rsqrt(jnp.mean(o * o, axis=-1, keepdims=True) + EPS) * on_ref[...]
        gated.append((o_n * _silu(r_ref[0, :, vsl].astype(F32))).astype(BF16))
    o_ref[0] = x_ref[0] + g_ref[0] * _dot(jnp.concatenate(gated, axis=-1), wo_ref[...])


def gla_mix(qk, v, r, la, o_norm, w_out, x, g, *, heads, tt=256):
    b, t, dk2 = qk.shape
    dk = dk2 // 2
    dv = v.shape[-1]
    d = x.shape[-1]
    hk, hv = dk // heads, dv // heads
    tri = jnp.kron(jnp.eye(tt // GLA_CHUNK, dtype=F32), jnp.tri(GLA_CHUNK, dtype=F32)).astype(BF16)
    body = functools.partial(_gla_body, heads=heads, hk=hk, hv=hv, n_chunks=tt // GLA_CHUNK)
    row = lambda i, j: (i, j, 0)
    return pl.pallas_call(
        body,
        out_shape=jax.ShapeDtypeStruct((b, t, d), F32),
        grid=(b, t // tt),
        in_specs=[
            pl.BlockSpec((1, tt, dk2), row),
            pl.BlockSpec((1, tt, dv), row),
            pl.BlockSpec((1, tt, dk), row),
            pl.BlockSpec((1, tt, dv), row),
            pl.BlockSpec((1, hv), lambda i, j: (0, 0)),
            pl.BlockSpec((tt, tt), lambda i, j: (0, 0)),
            pl.BlockSpec((dv, d), lambda i, j: (0, 0)),
            pl.BlockSpec((1, tt, d), row),
            pl.BlockSpec((1, 1, d), lambda i, j: (i, 0, 0)),
        ],
        out_specs=pl.BlockSpec((1, tt, d), row),
        scratch_shapes=[pltpu.VMEM((heads, hv, hk), F32)],
        compiler_params=_cparams(("parallel", "arbitrary")),
        name="gla_mix",
    )(qk, v, la, r, o_norm, tri, w_out, x, g)


def _ffn_body(x_ref, gain_ref, sc_ref, sh_ref, g_ref, wg_ref, wu_ref, wd_ref, o_ref, h_ref, acc_ref):
    f = pl.program_id(2)

    @pl.when(f == 0)
    def _():
        h_ref[...] = _norm_mod(x_ref[0], gain_ref[...], sc_ref[0], sh_ref[0]).astype(BF16)
        acc_ref[...] = jnp.zeros_like(acc_ref)

    h = h_ref[...]
    a = (_silu(_dot(h, wg_ref[...])) * _dot(h, wu_ref[...])).astype(BF16)
    acc_ref[...] += _dot(a, wd_ref[...])

    @pl.when(f == pl.num_programs(2) - 1)
    def _():
        o_ref[0] = x_ref[0] + g_ref[0] * acc_ref[...]


def ffn_dense(x, gain, sc, sh, g, w_gate, w_up, w_down, tm=1024, tf=512):
    b, t, d = x.shape
    ff = w_gate.shape[1]
    return pl.pallas_call(
        _ffn_body,
        out_shape=jax.ShapeDtypeStruct((b, t, d), F32),
        grid=(b, t // tm, ff // tf),
        in_specs=[
            pl.BlockSpec((1, tm, d), lambda i, j, f: (i, j, 0)),
            pl.BlockSpec((1, d), lambda i, j, f: (0, 0)),
            pl.BlockSpec((1, 1, d), lambda i, j, f: (i, 0, 0)),
            pl.BlockSpec((1, 1, d), lambda i, j, f: (i, 0, 0)),
            pl.BlockSpec((1, 1, d), lambda i, j, f: (i, 0, 0)),
            pl.BlockSpec((d, tf), lambda i, j, f: (0, f)),
            pl.BlockSpec((d, tf), lambda i, j, f: (0, f)),
            pl.BlockSpec((tf, d), lambda i, j, f: (f, 0)),
        ],
        out_specs=pl.BlockSpec((1, tm, d), lambda i, j, f: (i, j, 0)),
        scratch_shapes=[pltpu.VMEM((tm, d), BF16), pltpu.VMEM((tm, d), F32)],
        compiler_params=_cparams(("parallel", "parallel", "arbitrary")),
        name="ffn_dense",
    )(x, gain, sc, sh, g, w_gate, w_up, w_down)


def _group_rms(x, gmean):
    sq = x * x
    hi = sq.astype(BF16)
    lo = (sq - hi.astype(F32)).astype(BF16)
    cols = gmean.shape[0]
    parts = []
    for s in range(0, x.shape[-1], cols):
        parts.append(_dot(hi[:, s:s + cols], gmean) + _dot(lo[:, s:s + cols], gmean))
    ms = jnp.concatenate(parts, axis=-1)
    return x * lax.rsqrt(ms + EPS)


def _fox_in_body(x_ref, gain_ref, sc_ref, sh_ref, w_ref, bf_ref, qn_ref, kn_ref, gm_ref, tri_ref, pf_ref,
                 q_ref, k_ref, kf_ref, v_ref, sg_ref, carry_ref, *, d, heads):
    @pl.when(pl.program_id(1) == 0)
    def _():
        carry_ref[...] = jnp.zeros_like(carry_ref)

    h = _norm_mod(x_ref[0], gain_ref[...], sc_ref[0], sh_ref[0]).astype(BF16)
    gm = gm_ref[...]
    q = _group_rms(_dot(h, w_ref[:, 0:d]), gm) * qn_ref[...]
    q_ref[0] = (q * (FOX_HEAD_DIM ** -0.5 * LOG2E)).astype(BF16)
    k = _group_rms(_dot(h, w_ref[:, d:2 * d]), gm) * kn_ref[...]
    k_ref[0] = k.astype(BF16)
    v_ref[0, 0] = _dot(h, w_ref[:, 2 * d:3 * d]).T.astype(BF16)
    sg_ref[0] = _sigmoid(_dot(h, w_ref[:, 3 * d:4 * d])).astype(BF16)
    log_f = _log_sigmoid(_dot(h, w_ref[:, 4 * d:]) + bf_ref[...])
    fcum = _exact_left_dot(tri_ref[...], log_f) + carry_ref[...]
    carry_ref[...] = fcum[fcum.shape[0] - 1:, :]
    lane = lax.broadcasted_iota(jnp.int32, fcum.shape, 1)
    p1, p2, p3 = _split3(jnp.where(lane < heads, fcum * (-LOG2E), 0.0))
    packed = (p1.astype(F32) + pltpu.roll(p2.astype(F32), heads, 1)
              + pltpu.roll(p3.astype(F32), 2 * heads, 1)).astype(BF16)
    kf_ref[0] = _dot(packed, pf_ref[...]).astype(BF16)


def _fox_bias_placement(heads, d):
    src, dst = [], []
    for i in range(3):
        for hd in range(heads):
            src.append(i * heads + hd)
            dst.append((hd // 2) * LANES + 3 * (hd % 2) + i)
    return jnp.zeros((LANES, d), F32).at[jnp.array(src), jnp.array(dst)].set(1.0).astype(BF16)


def fox_in_proj(x, gain, sc, sh, w_pad, bf_pad, qn_t, kn_t, heads, tm=512):
    b, t, d = x.shape
    n_pad = w_pad.shape[1]
    gcols = 2 * LANES
    gm = (jnp.kron(jnp.eye(gcols // FOX_HEAD_DIM, dtype=F32),
                   jnp.ones((FOX_HEAD_DIM, FOX_HEAD_DIM), F32)) / FOX_HEAD_DIM).astype(BF16)
    tri = jnp.tri(tm, dtype=BF16)
    pf = _fox_bias_placement(heads, d)
    body = functools.partial(_fox_in_body, d=d, heads=heads)
    bf = lambda s: jax.ShapeDtypeStruct(s, BF16)
    row = lambda i, j: (i, j, 0)
    const = lambda i, j: (0, 0)
    return pl.pallas_call(
        body,
        out_shape=(bf((b, t, d)), bf((b, t, d)), bf((b, t, d)), bf((b, t // tm, d, tm)), bf((b, t, d))),
        grid=(b, t // tm),
        in_specs=[
            pl.BlockSpec((1, tm, d), row),
            pl.BlockSpec((1, d), const),
            pl.BlockSpec((1, 1, d), lambda i, j: (i, 0, 0)),
            pl.BlockSpec((1, 1, d), lambda i, j: (i, 0, 0)),
            pl.BlockSpec((d, n_pad), const),
            pl.BlockSpec((1, LANES), const),
            pl.BlockSpec((1, d), const),
            pl.BlockSpec((1, d), const),
            pl.BlockSpec((gcols, gcols), const),
            pl.BlockSpec((tm, tm), const),
            pl.BlockSpec((LANES, d), const),
        ],
        out_specs=(pl.BlockSpec((1, tm, d), row), pl.BlockSpec((1, tm, d), row), pl.BlockSpec((1, tm, d), row),
                   pl.BlockSpec((1, 1, d, tm), lambda i, j: (i, j, 0, 0)), pl.BlockSpec((1, tm, d), row)),
        scratch_shapes=[pltpu.VMEM((1, LANES), F32)],
        compiler_params=_cparams(("parallel", "arbitrary")),
        name="fox_in_proj",
    )(x, gain, sc, sh, w_pad, bf_pad, qn_t, kn_t, gm, tri, pf)


def _fox_attn_body(q_ref, k_ref, kf_ref, vt_ref, sg_ref, o_ref, s0_ref, s1_ref, mx0_ref, mx1_ref,
                   m_ref, l_ref, acc_ref, *, tq, tk):
    i = pl.program_id(2)
    hd = FOX_HEAD_DIM
    lane = lax.broadcasted_iota(jnp.int32, (1, LANES), 1)
    q2 = q_ref[0]
    qs = []
    for hh in range(2):
        qh = jnp.where((lane >= hh * hd) & (lane < (hh + 1) * hd), q2, jnp.zeros_like(q2))
        ones = jnp.where((lane >= 3 * hh) & (lane < 3 * hh + 3), 1.0, 0.0).astype(BF16)
        qs.append(jnp.concatenate([qh, jnp.broadcast_to(ones, qh.shape)], axis=-1))

    m_ref[...] = jnp.full_like(m_ref, -jnp.inf)
    l_ref[...] = jnp.zeros_like(l_ref)
    acc_ref[...] = jnp.zeros_like(acc_ref)

    def scores(j, s_ref, mx_ref, diagonal):
        ks = pl.multiple_of(j * tk, tk)
        kcat = jnp.concatenate([k_ref[0, pl.ds(ks, tk), :], kf_ref[0, pl.ds(ks, tk), :]], axis=-1)
        for hh in range(2):
            st = _dot_nt(kcat, qs[hh])
            if diagonal:
                key = lax.broadcasted_iota(jnp.int32, (tk, tq), 0)
                qry = lax.broadcasted_iota(jnp.int32, (tk, tq), 1)
                st = jnp.where(key <= qry, st, -jnp.inf)
            s_ref[hh] = st
            mx_ref[hh] = jnp.max(st, axis=0, keepdims=True)

    def consume(j, s_ref, mx_ref):
        vt = vt_ref[0, j]
        for hh in range(2):
            m_prev = m_ref[hh]
            m_new = jnp.maximum(m_prev, mx_ref[hh])
            alpha = jnp.exp2(m_prev - m_new)
            p = jnp.exp2(s_ref[hh] - m_new)
            l_ref[hh] = alpha * l_ref[hh] + jnp.sum(p, axis=0, keepdims=True)
            acc_ref[hh] = alpha * acc_ref[hh] + _dot(vt, p.astype(BF16))
            m_ref[hh] = m_new

    scores(i, s0_ref, mx0_ref, True)

    def pair(u, carry):
        t = 2 * u
        scores(t, s1_ref, mx1_ref, False)
        consume(jnp.where(u == 0, i, t - 1), s0_ref, mx0_ref)
        scores(t + 1, s0_ref, mx0_ref, False)
        consume(t, s1_ref, mx1_ref)
        return carry

    lax.fori_loop(0, i // 2, pair, 0)

    @pl.when(i % 2 == 1)
    def _():
        scores(i - 1, s1_ref, mx1_ref, False)
        consume(jnp.where(i == 1, i, i - 2), s0_ref, mx0_ref)
        consume(i - 1, s1_ref, mx1_ref)

    @pl.when(i % 2 == 0)
    def _():
        consume(jnp.maximum(i - 1, 0), s0_ref, mx0_ref)

    chan = lax.broadcasted_iota(jnp.int32, (2 * hd, 1), 0)
    o_t = jnp.where(chan < hd, acc_ref[0] * (1.0 / l_ref[0]), acc_ref[1] * (1.0 / l_ref[1]))
    o_ref[0] = (o_t.T * sg_ref[0].astype(F32)).astype(BF16)


def fox_attention(q, k, kf, vt, sg, tq=512):
    b, t, d = q.shape
    pw = 2 * FOX_HEAD_DIM
    pairs = d // pw
    nk, tk = vt.shape[1], vt.shape[3]
    assert tk == tq
    body = functools.partial(_fox_attn_body, tq=tq, tk=tk)
    qmap = lambda bi, p, i: (bi, i, p)
    kmap = lambda bi, p, i: (bi, 0, p)
    return pl.pallas_call(
        body,
        out_shape=jax.ShapeDtypeStruct((b, t, d), BF16),
        grid=(b, pairs, t // tq),
        in_specs=[
            pl.BlockSpec((1, tq, pw), qmap),
            pl.BlockSpec((1, t, pw), kmap),
            pl.BlockSpec((1, t, pw), kmap),
            pl.BlockSpec((1, nk, pw, tk), lambda bi, p, i: (bi, 0, p, 0)),
            pl.BlockSpec((1, tq, pw), qmap),
        ],
        out_specs=pl.BlockSpec((1, tq, pw), qmap),
        scratch_shapes=[
            pltpu.VMEM((2, tk, tq), F32),
            pltpu.VMEM((2, tk, tq), F32),
            pltpu.VMEM((2, 1, tq), F32),
            pltpu.VMEM((2, 1, tq), F32),
            pltpu.VMEM((2, 1, tq), F32),
            pltpu.VMEM((2, 1, tq), F32),
            pltpu.VMEM((2, pw, tq), F32),
        ],
        compiler_params=_cparams(("parallel", "parallel", "arbitrary")),
        name="fox_attention",
    )(q, k, kf, vt, sg)


META_E1, META_E2, META_W1, META_W2, META_R1, META_R2 = range(6)


def _lane_pick(a, lane, idx):
    return jnp.sum(jnp.where(lane == idx, a, 0.0), axis=-1, keepdims=True)


def _router_body(a_ref, wo_ref, g1_ref, x_ref, gain_ref, sc_ref, sh_ref, wr_ref, br_ref, tri_ref,
                 xo_ref, hp_ref, meta_ref, cnt_ref, carry_ref):
    @pl.when((pl.program_id(0) == 0) & (pl.program_id(1) == 0))
    def _():
        carry_ref[...] = jnp.zeros_like(carry_ref)

    x = x_ref[0] + g1_ref[0] * _dot(a_ref[0], wo_ref[...])
    xo_ref[0] = x
    h = _norm_mod(x, gain_ref[...], sc_ref[0], sh_ref[0])
    half = h.shape[-1] // 2
    hb = h.astype(BF16).astype(F32)
    lo = lax.bitcast_convert_type(hb[:, :half], jnp.uint32) >> 16
    hi = lax.bitcast_convert_type(hb[:, half:], jnp.uint32) & jnp.uint32(0xFFFF0000)
    hp_ref[0] = hi | lo

    h1 = h.astype(BF16)
    h2 = (h - h1.astype(F32)).astype(BF16)
    wr = wr_ref[...]
    w1 = wr.astype(BF16)
    w2 = (wr - w1.astype(F32)).astype(BF16)
    logits = _dot(h1, w1) + _dot(h1, w2) + _dot(h2, w1) + br_ref[...]
    lane = lax.broadcasted_iota(jnp.int32, logits.shape, 1)
    logits = jnp.where(lane < N_EXPERTS, logits, -jnp.inf)
    m1 = jnp.max(logits, axis=-1, keepdims=True)
    i1 = jnp.min(jnp.where(logits == m1, lane, LANES), axis=-1, keepdims=True)
    rest = jnp.where(lane == i1, -jnp.inf, logits)
    m2 = jnp.max(rest, axis=-1, keepdims=True)
    i2 = jnp.min(jnp.where(rest == m2, lane, LANES), axis=-1, keepdims=True)
    e2 = jnp.exp(m2 - m1)
    w1 = 1.0 / (1.0 + e2)
    w2 = e2 / (1.0 + e2)

    onehot = jnp.where(lane == i1, 1.0, 0.0) + jnp.where(lane == i2, 1.0, 0.0)
    before = _dot(tri_ref[...], onehot.astype(BF16)) + carry_ref[...]
    r1 = _lane_pick(before, lane, i1)
    r2 = _lane_pick(before, lane, i2)
    last = onehot.shape[0] - 1
    total = before[last:, :] + onehot[last:, :]
    carry_ref[...] = total
    cnt_ref[...] = total

    meta = jnp.zeros_like(logits)
    for slot, val in ((META_E1, i1.astype(F32)), (META_E2, i2.astype(F32)), (META_W1, w1), (META_W2, w2),
                      (META_R1, r1), (META_R2, r2)):
        meta = jnp.where(lane == slot, val, meta)
    meta_ref[0] = meta


def moe_router(a, w_out, g1, x, gain, sc, sh, wr_pad, br_pad, tm=512):
    b, t, d = x.shape
    row = lambda i, j: (i, j, 0)
    const = lambda i, j: (0, 0)
    tri = jnp.tri(tm, k=-1, dtype=BF16)
    return pl.pallas_call(
        _router_body,
        out_shape=(jax.ShapeDtypeStruct((b, t, d), F32),
                   jax.ShapeDtypeStruct((b, t, d // 2), jnp.uint32),
                   jax.ShapeDtypeStruct((b, t, LANES), F32),
                   jax.ShapeDtypeStruct((1, LANES), F32)),
        grid=(b, t // tm),
        in_specs=[
            pl.BlockSpec((1, tm, a.shape[-1]), row),
            pl.BlockSpec(w_out.shape, const),
            pl.BlockSpec((1, 1, d), lambda i, j: (i, 0, 0)),
            pl.BlockSpec((1, tm, d), row),
            pl.BlockSpec((1, d), const),
            pl.BlockSpec((1, 1, d), lambda i, j: (i, 0, 0)),
            pl.BlockSpec((1, 1, d), lambda i, j: (i, 0, 0)),
            pl.BlockSpec((d, LANES), const),
            pl.BlockSpec((1, LANES), const),
            pl.BlockSpec((tm, tm), const),
        ],
        out_specs=(pl.BlockSpec((1, tm, d), row), pl.BlockSpec((1, tm, d // 2), row),
                   pl.BlockSpec((1, tm, LANES), row), pl.BlockSpec((1, LANES), const)),
        scratch_shapes=[pltpu.VMEM((1, LANES), F32)],
        compiler_params=_cparams(("arbitrary", "arbitrary")),
        name="moe_router",
    )(a, w_out, g1, x, gain, sc, sh, wr_pad, br_pad, tri)


def _row_copy(src_ref, src_row, dst_ref, dst_row, sem):
    return pltpu.make_async_copy(src_ref.at[pl.ds(src_row, 1), :], dst_ref.at[pl.ds(dst_row, 1), :], sem)


def _dispatch_body(dest_ref, hp_ref, xs_in_ref, xs_ref, sem, *, tm):
    del xs_in_ref

    for r in range(tm):
        _row_copy(hp_ref, r, xs_ref, dest_ref[0, 0, r], sem).start(priority=0)
        _row_copy(hp_ref, r, xs_ref, dest_ref[0, 0, tm + r], sem).start(priority=1)

    def drain(r, carry):
        _row_copy(hp_ref, r, xs_ref, dest_ref[0, 0, r], sem).wait()
        _row_copy(hp_ref, r, xs_ref, dest_ref[0, 0, tm + r], sem).wait()
        return carry

    lax.fori_loop(0, tm, drain, 0, unroll=8)


def moe_dispatch(hp, dest, rows, tm):
    n, half = hp.shape
    body = functools.partial(_dispatch_body, tm=tm)
    return pl.pallas_call(
        body,
        out_shape=jax.ShapeDtypeStruct((rows, half), jnp.uint32),
        grid=(n // tm,),
        in_specs=[
            pl.BlockSpec((1, 1, 2 * tm), lambda i: (i, 0, 0), memory_space=pltpu.SMEM),
            pl.BlockSpec((tm, half), lambda i: (i, 0)),
            pl.BlockSpec(memory_space=pl.ANY),
        ],
        out_specs=pl.BlockSpec(memory_space=pl.ANY),
        scratch_shapes=[pltpu.SemaphoreType.DMA],
        input_output_aliases={2: 0},
        compiler_params=pltpu.CompilerParams(dimension_semantics=("arbitrary",), vmem_limit_bytes=VMEM_LIMIT,
                                             disable_bounds_checks=True),
        name="moe_dispatch",
    )(dest, hp, jnp.zeros((rows, half), jnp.uint32))


def _experts_body(te_ref, nu_ref, tv_ref, xs_ref, wg_ref, wu_ref, wd_ref, y_ref, h_ref, acc_ref):
    del te_ref
    i = pl.program_id(0)
    f = pl.program_id(1)
    half_rows = h_ref.shape[0] // 2

    @pl.when(i < nu_ref[0])
    def _():
        @pl.when(f == 0)
        def _():
            xp = xs_ref[...]
            lo = lax.bitcast_convert_type(xp << 16, F32).astype(BF16)
            hi = lax.bitcast_convert_type(xp & jnp.uint32(0xFFFF0000), F32).astype(BF16)
            h_ref[...] = jnp.concatenate([lo, hi], axis=-1)
            acc_ref[...] = jnp.zeros_like(acc_ref)

        def swiglu_rows(rows):
            h = h_ref[rows, :]
            a = (_silu(_dot(h, wg_ref[0, 0].astype(BF16))) * _dot(h, wu_ref[0, 0].astype(BF16))).astype(BF16)
            acc_ref[rows, :] += _dot(a, wd_ref[0, 0].astype(BF16))

        upper_used = tv_ref[i] > half_rows

        @pl.when(upper_used)
        def _():
            swiglu_rows(slice(None))

        @pl.when(jnp.logical_not(upper_used))
        def _():
            swiglu_rows(slice(0, half_rows))

        @pl.when(f == pl.num_programs(1) - 1)
        def _():
            y_ref[...] = acc_ref[...]

    @pl.when((i >= nu_ref[0]) & (f == pl.num_programs(1) - 1))
    def _():
        y_ref[...] = jnp.zeros_like(y_ref)


def moe_experts(xs, tile_expert, n_used, tile_valid, w_gate, w_up, w_down, layer, tr, tf=512):
    rows, half = xs.shape
    _, ne, d, ff = w_gate.shape
    nf = ff // tf
    tile = lambda i, nu: jnp.maximum(jnp.minimum(i, nu[0] - 1), 0)
    fblk = lambda i, f, nu: jnp.where(i < nu[0], f, nf - 1)
    return pl.pallas_call(
        _experts_body,
        out_shape=jax.ShapeDtypeStruct((rows, d), F32),
        grid_spec=pltpu.PrefetchScalarGridSpec(
            num_scalar_prefetch=3,
            grid=(rows // tr, nf),
            in_specs=[
                pl.BlockSpec((tr, half), lambda i, f, te, nu, tv: (tile(i, nu), 0)),
                pl.BlockSpec((1, 1, d, tf),
                             lambda i, f, te, nu, tv: (layer, te[tile(i, nu)], 0, fblk(i, f, nu))),
                pl.BlockSpec((1, 1, d, tf),
                             lambda i, f, te, nu, tv: (layer, te[tile(i, nu)], 0, fblk(i, f, nu))),
                pl.BlockSpec((1, 1, tf, d),
                             lambda i, f, te, nu, tv: (layer, te[tile(i, nu)], fblk(i, f, nu), 0)),
            ],
            out_specs=pl.BlockSpec((tr, d), lambda i, f, te, nu, tv: (i, 0)),
            scratch_shapes=[pltpu.VMEM((tr, d), BF16), pltpu.VMEM((tr, d), F32)],
        ),
        compiler_params=_cparams(("arbitrary", "arbitrary")),
        name="moe_experts",
    )(tile_expert, n_used, tile_valid, xs, w_gate, w_up, w_down)


def _combine_body(dest_ref, nxt_ref, x_ref, g_ref, meta_ref, y_ref, o_ref, buf_ref, sem, *, tm):
    i = pl.program_id(0)
    n = pl.num_programs(0)

    def copies(idx_ref, slot, r):
        return (_row_copy(y_ref, idx_ref[0, 0, r], buf_ref.at[slot, 0], r, sem.at[slot]),
                _row_copy(y_ref, idx_ref[0, 0, tm + r], buf_ref.at[slot, 1], r, sem.at[slot]))

    def start_gather(idx_ref, slot):
        for r in range(tm):
            c0, c1 = copies(idx_ref, slot, r)
            c0.start(priority=0)
            c1.start(priority=1)

    @pl.when(i == 0)
    def _():
        start_gather(dest_ref, 0)

    @pl.when(i + 1 < n)
    def _():
        start_gather(nxt_ref, (i + 1) % 2)

    slot = i % 2

    def drain(r, carry):
        c0, c1 = copies(dest_ref, slot, r)
        c0.wait()
        c1.wait()
        return carry

    lax.fori_loop(0, tm, drain, 0, unroll=8)

    meta = meta_ref[...]
    lane = lax.broadcasted_iota(jnp.int32, meta.shape, 1)
    w1 = _lane_pick(meta, lane, META_W1)
    w2 = _lane_pick(meta, lane, META_W2)
    o_ref[...] = x_ref[...] + g_ref[0] * (w1 * buf_ref[slot, 0] + w2 * buf_ref[slot, 1])


def moe_combine(x2, g, meta2, y, dest, t, tm):
    n, d = x2.shape
    per_batch = t // tm
    steps = n // tm
    body = functools.partial(_combine_body, tm=tm)
    idx_block = lambda imap: pl.BlockSpec((1, 1, 2 * tm), imap, memory_space=pltpu.SMEM)
    return pl.pallas_call(
        body,
        out_shape=jax.ShapeDtypeStruct((n, d), F32),
        grid=(steps,),
        in_specs=[
            idx_block(lambda i: (i, 0, 0)),
            idx_block(lambda i: (jnp.minimum(i + 1, steps - 1), 0, 0)),
            pl.BlockSpec((tm, d), lambda i: (i, 0)),
            pl.BlockSpec((1, 1, d), lambda i: (i // per_batch, 0, 0)),
            pl.BlockSpec((tm, LANES), lambda i: (i, 0)),
            pl.BlockSpec(memory_space=pl.ANY),
        ],
        out_specs=pl.BlockSpec((tm, d), lambda i: (i, 0)),
        scratch_shapes=[pltpu.VMEM((2, 2, tm, d), F32), pltpu.SemaphoreType.DMA((2,))],
        compiler_params=pltpu.CompilerParams(dimension_semantics=("arbitrary",), vmem_limit_bytes=VMEM_LIMIT,
                                             disable_bounds_checks=True),
        name="moe_combine",
    )(dest, dest, x2, g, meta2, y)


def moe_plan(meta2, counts, tr, tm):
    n = meta2.shape[0]
    e1 = meta2[:, META_E1].astype(jnp.int32)
    e2 = meta2[:, META_E2].astype(jnp.int32)
    cnt = counts[0, :N_EXPERTS].astype(jnp.int32)
    padded = (cnt + tr - 1) // tr * tr
    ends = jnp.cumsum(padded)
    starts = ends - padded
    experts = jnp.arange(N_EXPERTS, dtype=jnp.int32)

    def group_start(e):
        return jnp.sum(jnp.where(e[:, None] == experts[None, :], starts[None, :], 0), axis=1)

    d1 = group_start(e1) + meta2[:, META_R1].astype(jnp.int32)
    d2 = group_start(e2) + meta2[:, META_R2].astype(jnp.int32)
    dest = jnp.concatenate([d1.reshape(n // tm, 1, tm), d2.reshape(n // tm, 1, tm)], axis=-1)
    n_tiles = 2 * n // tr + N_EXPERTS
    tile_start = jnp.arange(n_tiles, dtype=jnp.int32) * tr
    tile_expert = jnp.sum((tile_start[:, None] >= ends[None, :]).astype(jnp.int32), axis=1)
    tile_expert = jnp.minimum(tile_expert, N_EXPERTS - 1)
    is_tile_expert = tile_expert[:, None] == experts[None, :]
    group_end = jnp.sum(jnp.where(is_tile_expert, (starts + cnt)[None, :], 0), axis=1)
    tile_valid = jnp.clip(group_end - tile_start, 0, tr).astype(jnp.int32)
    n_used = (ends[-1:] // tr).astype(jnp.int32)
    return dest, tile_expert, n_used, tile_valid, n_tiles * tr


def _pad_cols(w, n):
    return jnp.pad(w, ((0, 0), (0, n - w.shape[1])))


def kernel(x, c, ada_w, ada_b, norm_gain, gla_w_in, gla_w_gate_up, gla_b_gate, gla_o_norm, gla_w_out,
           fox_w_in, fox_b_f, fox_q_norm, fox_k_norm, fox_w_out, ffn_w_gate, ffn_w_up, ffn_w_down,
           moe_w_router, moe_b_router, moe_w_gate, moe_w_up, moe_w_down):
    b, t, d = x.shape
    depth = ada_w.shape[0]
    gla_rank = gla_w_gate_up.shape[1]
    dk = gla_w_gate_up.shape[2]
    dv = gla_w_out.shape[1]
    fox_heads = fox_b_f.shape[1]

    c_pad = jnp.pad(c, ((0, 8 - b), (0, 0)))
    mods = ada_mods(c_pad, ada_w, ada_b)[:, :b]
    mods = mods.reshape(depth, b, 6, 1, d)

    for i in range(depth):
        j = i // 2
        sh1, sc1, g1, sh2, sc2, g2 = (mods[i, :, m] for m in range(6))
        gain1 = norm_gain[i, 0].reshape(1, d)
        gain2 = norm_gain[i, 1].reshape(1, d)
        if i % 2 == 0:
            n_main = 2 * dk + 2 * dv
            w_pad = _pad_cols(gla_w_in[j], n_main + LANES).astype(BF16)
            wg_pad = jnp.pad(gla_w_gate_up[j], ((0, LANES - gla_rank), (0, 0))).astype(BF16)
            qk, v, r, la = gla_in_proj(x, gain1, sc1, sh1, w_pad, wg_pad, gla_b_gate[j].reshape(1, dk),
                                       dk=dk, dv=dv)
            x = gla_mix(qk, v, r, la, gla_o_norm[j].reshape(1, -1), gla_w_out[j].astype(BF16), x, g1,
                        heads=GLA_HEADS)
            x = ffn_dense(x, gain2, sc2, sh2, g2, ffn_w_gate[j].astype(BF16), ffn_w_up[j].astype(BF16),
                          ffn_w_down[j].astype(BF16))
        else:
            w_pad = _pad_cols(fox_w_in[j], 4 * d + LANES).astype(BF16)
            bf_pad = jnp.pad(fox_b_f[j], (0, LANES - fox_heads)).reshape(1, LANES)
            qn_t = jnp.tile(fox_q_norm[j], fox_heads).reshape(1, d)
            kn_t = jnp.tile(fox_k_norm[j], fox_heads).reshape(1, d)
            q, k, kf, v, sg = fox_in_proj(x, gain1, sc1, sh1, w_pad, bf_pad, qn_t, kn_t, fox_heads)
            a = fox_attention(q, k, kf, v, sg)
            wr_pad = _pad_cols(moe_w_router[j], LANES)
            br_pad = jnp.pad(moe_b_router[j], (0, LANES - N_EXPERTS)).reshape(1, LANES)
            x, hp, meta, counts = moe_router(a, fox_w_out[j].astype(BF16), g1, x, gain2, sc2, sh2,
                                             wr_pad, br_pad)
            meta2 = meta.reshape(b * t, LANES)
            dest, tile_expert, n_used, tile_valid, rows = moe_plan(meta2, counts, MOE_ROW_TILE, MOE_TOKEN_TILE)
            xs = moe_dispatch(hp.reshape(b * t, d // 2), dest, rows, MOE_TOKEN_TILE)
            y = moe_experts(xs, tile_expert, n_used, tile_valid, moe_w_gate, moe_w_up, moe_w_down, j,
                            MOE_ROW_TILE)
            x = moe_combine(x.reshape(b * t, d), g2, meta2, y, dest, t, MOE_TOKEN_TILE).reshape(b, t, d)
    return x
```
